```python
import math
import jax, jax.numpy as jnp
from jax import lax
import numpy as np

D_MODEL = 1024
BATCH = 32
SEQ = 2048
DEPTH = 2
DEC_BATCH = 16
DEC_SEQ = 64
PAST_LEN = 4096

CHUNK = 64
N_MLSTM_LAYERS = (DEPTH + 1) // 2
N_POOL_LAYERS = DEPTH // 2
MLSTM_HEADS = 4
MLSTM_HEAD_DIM = D_MODEL // MLSTM_HEADS
POOL_WINDOWS = (2, 4, 8, 16)
POOL_GROUP = D_MODEL // len(POOL_WINDOWS)
POOL_STATE = max(POOL_WINDOWS) - 1
N_MEM = 256
XATTN_HEADS = 4
XATTN_HEAD_DIM = D_MODEL // XATTN_HEADS
PEER_HEADS = 8
PEER_N_KEYS = 128
PEER_EXPERTS = PEER_N_KEYS * PEER_N_KEYS
PEER_QUERY_DIM = 256
PEER_HALF = PEER_QUERY_DIM // 2
PEER_TOPK = 16
PEER_BLOCK = 256
EPS = 1e-6

kernel_name = "streaming_mlstm_pool_peer_step"

F32 = jnp.float32


def rmsnorm(x, g):
    xf = x.astype(F32)
    y = xf * lax.rsqrt(jnp.mean(xf * xf, axis=-1, keepdims=True) + EPS) * g.astype(F32)
    return y.astype(x.dtype)


def mlstm_project(u, w_in, b_i, b_f):
    B, T, _ = u.shape
    D, H, dh = D_MODEL, MLSTM_HEADS, MLSTM_HEAD_DIM
    z = u @ w_in
    q = z[..., :D].reshape(B, T, H, dh)
    k = z[..., D:2 * D].reshape(B, T, H, dh) * (dh ** -0.5)
    v = z[..., 2 * D:3 * D].reshape(B, T, H, dh)
    o = z[..., 3 * D:4 * D]
    ig = z[..., 4 * D:4 * D + H].astype(F32) + b_i.astype(F32)
    lf = jax.nn.log_sigmoid(z[..., 4 * D + H:].astype(F32) + b_f.astype(F32))
    return q, k, v, o, ig, lf


def mlstm_chunk(carry, inp):
    C, n, m = carry
    q, k, v, ig, lf = inp
    L = q.shape[1]
    bt = jnp.cumsum(lf, axis=1).transpose(0, 2, 1)
    igt = ig.transpose(0, 2, 1)
    logw = bt[:, :, :, None] - bt[:, :, None, :] + igt[:, :, None, :]
    mask = jnp.tril(jnp.ones((L, L), dtype=bool))
    logw = jnp.where(mask, logw, -jnp.inf)
    inter = bt + m[:, :, None]
    m_t = jnp.maximum(inter, jnp.max(logw, axis=-1))
    w = jnp.exp(logw - m_t[..., None])
    a_inter = jnp.exp(inter - m_t)
    sw = jnp.einsum('blhd,bshd->bhls', q, k).astype(F32) * w
    num = (a_inter[..., None] * jnp.einsum('blhd,bhde->bhle', q, C)
           + jnp.einsum('bhls,bshe->bhle', sw, v))
    den = a_inter * jnp.einsum('blhd,bhd->bhl', q, n) + jnp.sum(sw, axis=-1)
    h = num / jnp.maximum(jnp.abs(den), jnp.exp(-m_t))[..., None]
    m_new = m_t[:, :, -1]
    g_state = jnp.exp(bt[:, :, -1] + m - m_new)
    w_s = jnp.exp(bt[:, :, -1:] - bt + igt - m_new[..., None])
    kw = k * w_s.transpose(0, 2, 1)[..., None]
    C_new = g_state[..., None, None] * C + jnp.einsum('bshd,bshe->bhde', kw, v)
    n_new = g_state[..., None] * n + jnp.sum(kw, axis=1)
    return (C_new, n_new, m_new), h.transpose(0, 2, 1, 3)


def mlstm_output(h, o, g_head, w_out):
    B, T, H, dv = h.shape
    hf = h.astype(F32)
    mu = jnp.mean(hf, axis=-1, keepdims=True)
    var = jnp.mean(jnp.square(hf - mu), axis=-1, keepdims=True)
    hn = ((hf - mu) * lax.rsqrt(var + EPS)).reshape(B, T, H * dv) * g_head.astype(F32)
    y = (hn * jax.nn.sigmoid(o.astype(F32))).astype(o.dtype)
    return y @ w_out


def mlstm_prompt(u, w_in, b_i, b_f, g_head, w_out):
    B, T, _ = u.shape
    nc = T // CHUNK
    H, dh = MLSTM_HEADS, MLSTM_HEAD_DIM
    q, k, v, o, ig, lf = mlstm_project(u, w_in, b_i, b_f)

    def to_chunks(a):
        return jnp.moveaxis(a.reshape((B, nc, CHUNK) + a.shape[2:]), 1, 0)

    init = (jnp.zeros((B, H, dh, dh), F32), jnp.zeros((B, H, dh), F32), jnp.zeros((B, H), F32))
    (C, n, m), h = lax.scan(mlstm_chunk, init,
                            (to_chunks(q), to_chunks(k), to_chunks(v), to_chunks(ig), to_chunks(lf)))
    h = jnp.moveaxis(h, 0, 1).reshape(B, T, H, dh)
    return mlstm_output(h, o, g_head, w_out), C, n, m


def mlstm_sample(u, C, n, m, w_in, b_i, b_f, g_head, w_out):
    q, k, v, o, ig, lf = mlstm_project(u, w_in, b_i, b_f)
    (C, n, m), h = mlstm_chunk((C.astype(F32), n.astype(F32), m.astype(F32)), (q, k, v, ig, lf))
    return mlstm_output(h, o, g_head, w_out), C, n, m


def pool_mix(u, prefix, pos0, w_pool, s_pool):
    B, T, D = u.shape
    P = POOL_STATE
    ext = jnp.concatenate([prefix.astype(F32), u.astype(F32)], axis=1)
    S = jnp.concatenate([jnp.zeros((B, 1, D), F32), jnp.cumsum(ext, axis=1)], axis=1)
    end = S[:, P + 1:P + 1 + T]
    pos = pos0 + jnp.arange(T)
    outs = []
    for g, w in enumerate(POOL_WINDOWS):
        lo, hi = g * POOL_GROUP, (g + 1) * POOL_GROUP
        start = S[:, P + 1 - w:P + 1 - w + T, lo:hi]
        cnt = jnp.minimum(pos + 1, w).astype(F32)[None, :, None]
        p = (end[..., lo:hi] - start) / cnt - u[..., lo:hi].astype(F32)
        outs.append(jnp.einsum('btc,cd->btd', p, w_pool[g].astype(F32)))
    y = jnp.concatenate(outs, axis=-1) * s_pool.astype(F32)
    return y.astype(u.dtype)


def mem_kv(mem, g_mem, w_k, w_v):
    B = mem.shape[0]
    mn = rmsnorm(mem, g_mem)
    k = (mn @ w_k).reshape(B, N_MEM, XATTN_HEADS, XATTN_HEAD_DIM)
    v = (mn @ w_v).reshape(B, N_MEM, XATTN_HEADS, XATTN_HEAD_DIM)
    return k, v


def cross_attend(u, k, v, w_q, w_o):
    B, T, _ = u.shape
    q = (u @ w_q).reshape(B, T, XATTN_HEADS, XATTN_HEAD_DIM)
    s = jnp.einsum('bthd,bmhd->bhtm', q, k).astype(F32) * (XATTN_HEAD_DIM ** -0.5)
    p = jax.nn.softmax(s, axis=-1).astype(v.dtype)
    o = jnp.einsum('bhtm,bmhd->bthd', p, v).reshape(B, T, D_MODEL)
    return o @ w_o


def peer(u, w_q, sub_keys, expert_u, expert_v):
    B, T, D = u.shape
    flat = u.reshape(B * T, D)
    N = flat.shape[0]
    nblk = -(-N // PEER_BLOCK)
    flat_p = jnp.pad(flat, ((0, nblk * PEER_BLOCK - N), (0, 0))).reshape(nblk, PEER_BLOCK, D)

    def block(xb):
        Tb = xb.shape[0]
        q = (xb @ w_q).reshape(Tb, PEER_HEADS, 2, PEER_HALF)
        s = jnp.einsum('thpc,hpnc->thpn', q, sub_keys).astype(F32)
        sv, si = lax.top_k(s, PEER_TOPK)
        cand = (sv[:, :, 0, :, None] + sv[:, :, 1, None, :]).reshape(Tb, PEER_HEADS, PEER_TOPK * PEER_TOPK)
        cidx = (si[:, :, 0, :, None] * PEER_N_KEYS + si[:, :, 1, None, :]).reshape(Tb, PEER_HEADS, PEER_TOPK * PEER_TOPK)
        top, pos = lax.top_k(cand, PEER_TOPK)
        eidx = jnp.take_along_axis(cidx, pos, axis=-1)
        gate = jax.nn.softmax(top, axis=-1)
        ue = expert_u[eidx]
        act = jax.nn.gelu(jnp.einsum('td,thkd->thk', xb, ue).astype(F32), approximate=False)
        ve = expert_v[eidx]
        return jnp.einsum('thk,thkd->td', (gate * act).astype(ve.dtype), ve).astype(xb.dtype)

    out = lax.map(block, flat_p).reshape(nblk * PEER_BLOCK, D)[:N]
    return out.reshape(B, T, D)


def setup_inputs(seed: int = 0) -> dict:
    key = jax.random.key(seed)
    ks = iter(jax.random.split(key, 40))
    D, H, dh = D_MODEL, MLSTM_HEADS, MLSTM_HEAD_DIM

    def nrm(shape, scale=1.0):
        return jax.random.normal(next(ks), shape, F32) * scale

    def gain(shape):
        return 1.0 + 0.02 * jax.random.normal(next(ks), shape, F32)

    return {
        "x_prompt": nrm((BATCH, SEQ, D)),
        "x_sample": nrm((DEC_BATCH, DEC_SEQ, D)),
        "state_mlstm_C": nrm((N_MLSTM_LAYERS, DEC_BATCH, H, dh, dh), 0.1),
        "state_mlstm_n": nrm((N_MLSTM_LAYERS, DEC_BATCH, H, dh), 0.1),
        "state_mlstm_m": nrm((N_MLSTM_LAYERS, DEC_BATCH, H)),
        "state_pool": nrm((N_POOL_LAYERS, DEC_BATCH, POOL_STATE, D)),
        "cache_mem_k": nrm((DEPTH, DEC_BATCH, N_MEM, XATTN_HEADS, XATTN_HEAD_DIM)),
        "cache_mem_v": nrm((DEPTH, DEC_BATCH, N_MEM, XATTN_HEADS, XATTN_HEAD_DIM)),
        "mem_prompt": nrm((BATCH, N_MEM, D)),
        "g_mix": gain((DEPTH, D)),
        "w_mlstm_in": nrm((N_MLSTM_LAYERS, D, 4 * D + 2 * H), D ** -0.5),
        "b_mlstm_i": nrm((N_MLSTM_LAYERS, H), 0.1),
        "b_mlstm_f": jnp.linspace(3.0, 6.0, H, dtype=F32)[None, :] + nrm((N_MLSTM_LAYERS, H), 0.1),
        "g_mlstm_head": gain((N_MLSTM_LAYERS, D)),
        "w_mlstm_out": nrm((N_MLSTM_LAYERS, D, D), D ** -0.5),
        "w_pool": nrm((N_POOL_LAYERS, len(POOL_WINDOWS), POOL_GROUP, POOL_GROUP), POOL_GROUP ** -0.5),
        "s_pool": gain((N_POOL_LAYERS, D)),
        "g_xattn": gain((DEPTH, D)),
        "g_mem": gain((DEPTH, D)),
        "w_xq": nrm((DEPTH, D, D), D ** -0.5),
        "w_xk": nrm((DEPTH, D, D), D ** -0.5),
        "w_xv": nrm((DEPTH, D, D), D ** -0.5),
        "w_xo": nrm((DEPTH, D, D), D ** -0.5),
        "g_ffn": gain((DEPTH, D)),
        "w_peer_q": nrm((DEPTH, D, PEER_HEADS * PEER_QUERY_DIM), D ** -0.5),
        "peer_keys": nrm((DEPTH, PEER_HEADS, 2, PEER_N_KEYS, PEER_HALF), PEER_HALF ** -0.5),
        "peer_u": nrm((DEPTH, PEER_EXPERTS, D), D ** -0.5),
        "peer_v": nrm((DEPTH, PEER_EXPERTS, D), PEER_HEADS ** -0.5),
        "g_final": gain((D,)),
    }


def reference(x_prompt, x_sample, state_mlstm_C, state_mlstm_n, state_mlstm_m, state_pool,
              cache_mem_k, cache_mem_v, mem_prompt, g_mix, w_mlstm_in, b_mlstm_i, b_mlstm_f,
              g_mlstm_head, w_mlstm_out, w_pool, s_pool, g_xattn, g_mem, w_xq, w_xk, w_xv, w_xo,
              g_ffn, w_peer_q, peer_keys, peer_u, peer_v, g_final):
    xp, xs = x_prompt, x_sample
    Bp = xp.shape[0]
    C_p, n_p, m_p, C_s, n_s, m_s = [], [], [], [], [], []
    pool_p, pool_s, mk_p, mv_p = [], [], [], []
    for i in range(DEPTH):
        j = i // 2
        up = rmsnorm(xp, g_mix[i])
        us = rmsnorm(xs, g_mix[i])
        if i % 2 == 0:
            yp, C, n, m = mlstm_prompt(up, w_mlstm_in[j], b_mlstm_i[j], b_mlstm_f[j],
                                       g_mlstm_head[j], w_mlstm_out[j])
            C_p.append(C); n_p.append(n); m_p.append(m)
            ys, C, n, m = mlstm_sample(us, state_mlstm_C[j], state_mlstm_n[j], state_mlstm_m[j],
                                       w_mlstm_in[j], b_mlstm_i[j], b_mlstm_f[j],
                                       g_mlstm_head[j], w_mlstm_out[j])
            C_s.append(C); n_s.append(n); m_s.append(m)
        else:
            prefix = jnp.zeros((Bp, POOL_STATE, D_MODEL), up.dtype)
            yp = pool_mix(up, prefix, 0, w_pool[j], s_pool[j])
            pool_p.append(up[:, -POOL_STATE:])
            ys = pool_mix(us, state_pool[j], PAST_LEN, w_pool[j], s_pool[j])
            pool_s.append(jnp.concatenate([state_pool[j].astype(us.dtype), us], axis=1)[:, -POOL_STATE:])
        xp = xp + yp
        xs = xs + ys
        kp, vp = mem_kv(mem_prompt, g_mem[i], w_xk[i], w_xv[i])
        mk_p.append(kp); mv_p.append(vp)
        xp = xp + cross_attend(rmsnorm(xp, g_xattn[i]), kp, vp, w_xq[i], w_xo[i])
        xs = xs + cross_attend(rmsnorm(xs, g_xattn[i]), cache_mem_k[i], cache_mem_v[i], w_xq[i], w_xo[i])
        xp = xp + peer(rmsnorm(xp, g_ffn[i]), w_peer_q[i], peer_keys[i], peer_u[i], peer_v[i])
        xs = xs + peer(rmsnorm(xs, g_ffn[i]), w_peer_q[i], peer_keys[i], peer_u[i], peer_v[i])
    y_prompt = rmsnorm(xp, g_final)
    y_sample = rmsnorm(xs, g_final)
    return (y_prompt, y_sample,
            jnp.stack(C_p), jnp.stack(n_p), jnp.stack(m_p), jnp.stack(pool_p),
            jnp.stack(mk_p), jnp.stack(mv_p),
            jnp.stack(C_s), jnp.stack(n_s), jnp.stack(m_s), jnp.stack(pool_s))
```

```python
import functools
import math

import jax
import jax.numpy as jnp
from jax import lax
from jax.experimental import pallas as pl
from jax.experimental.pallas import tpu as pltpu

F32 = jnp.float32
BF16 = jnp.bfloat16
I32 = jnp.int32
U32 = jnp.uint32
EPS = 1e-6
HIGHEST = lax.Precision.HIGHEST

D_MODEL = 1024
CHUNK = 64
HEADS = 4
HEAD_DIM = D_MODEL // HEADS
POOL_WINDOWS = (2, 4, 8, 16)
POOL_GROUP = D_MODEL // len(POOL_WINDOWS)
POOL_HALO = 16
N_MEM = 256
PEER_HEADS = 8
PEER_N_KEYS = 128
PEER_HALF = 128
PEER_TOPK = 16
PEER_SLOTS = PEER_HEADS * PEER_TOPK
LANES = 128
SUBLANES = 8
ROW_WORDS = D_MODEL // 2
ROW_SUBLANES = ROW_WORDS // LANES
VMEM_LIMIT = 56 * 1024 * 1024


def _cparams(sem, vmem=None):
    return pltpu.CompilerParams(dimension_semantics=sem, vmem_limit_bytes=vmem)


def _rms(x, g):
    return x * lax.rsqrt(jnp.mean(x * x, axis=-1, keepdims=True) + EPS) * g


def _norm_matmul_kernel(*refs, n_w, hi, emit_u, n_chunk):
    x_ref, g_ref = refs[0], refs[1]
    w_refs = refs[2:2 + n_w]
    o_refs = refs[2 + n_w:]
    u = _rms(x_ref[...], g_ref[...])
    ub = u.astype(BF16)
    for w_ref, o_ref, h in zip(w_refs, o_refs, hi):
        n = w_ref.shape[1]
        step = min(n, n_chunk)
        for j in range(0, n, step):
            if h:
                o = jnp.dot(u, w_ref[:, j:j + step], precision=HIGHEST, preferred_element_type=F32)
            else:
                o = jnp.dot(ub, w_ref[:, j:j + step], preferred_element_type=F32)
            o_ref[:, j:j + step] = o.astype(o_ref.dtype)
    if emit_u:
        o_refs[n_w][...] = u


def norm_matmul(x, g, ws, out_dtypes, hi=None, emit_u=False, tm=512):
    m, d = x.shape
    tm = min(tm, m)
    hi = tuple(hi or (False,) * len(ws))
    in_specs = [pl.BlockSpec((tm, d), lambda i: (i, 0)), pl.BlockSpec((1, d), lambda i: (0, 0))]
    in_specs += [pl.BlockSpec(w.shape, lambda i: (0, 0)) for w in ws]
    out_shape = [jax.ShapeDtypeStruct((m, w.shape[1]), dt) for w, dt in zip(ws, out_dtypes)]
    out_specs = [pl.BlockSpec((tm, w.shape[1]), lambda i: (i, 0)) for w in ws]
    if emit_u:
        out_shape.append(jax.ShapeDtypeStruct((m, d), F32))
        out_specs.append(pl.BlockSpec((tm, d), lambda i: (i, 0)))
    return pl.pallas_call(
        functools.partial(_norm_matmul_kernel, n_w=len(ws), hi=hi, emit_u=emit_u, n_chunk=512),
        grid=(m // tm,), in_specs=in_specs, out_specs=out_specs, out_shape=out_shape,
        compiler_params=_cparams(("parallel",), VMEM_LIMIT), name="norm_matmul",
    )(x, g.reshape(1, d), *ws)


def _rmsnorm_kernel(x_ref, g_ref, o_ref):
    o_ref[...] = _rms(x_ref[...], g_ref[...])


def rmsnorm(x, g, tm=512):
    m, d = x.shape
    tm = min(tm, m)
    return pl.pallas_call(
        _rmsnorm_kernel, grid=(m // tm,),
        in_specs=[pl.BlockSpec((tm, d), lambda i: (i, 0)), pl.BlockSpec((1, d), lambda i: (0, 0))],
        out_specs=pl.BlockSpec((tm, d), lambda i: (i, 0)),
        out_shape=jax.ShapeDtypeStruct((m, d), F32),
        compiler_params=_cparams(("parallel",)), name="final_rmsnorm",
    )(x, g.reshape(1, d))


def _log_sigmoid(x):
    return jnp.minimum(x, 0.0) - jnp.log1p(jnp.exp(-jnp.abs(x)))


def _mlstm_kernel(q_ref, k_ref, v_ref, gz_ref, bias_ref, c0_ref, n0_ref, m0_ref,
                  h_ref, c_out, n_out, m_out, c_scr, n_scr, m_scr):
    c = pl.program_id(1)
    L = q_ref.shape[1]

    @pl.when(c == 0)
    def _():
        c_scr[...] = c0_ref[0]
        n_scr[...] = n0_ref[0]
        m_scr[...] = m0_ref[0]

    row = lax.broadcasted_iota(I32, (L, L), 0)
    col = lax.broadcasted_iota(I32, (L, L), 1)
    tril = col <= row
    eye = col == row
    pre = gz_ref[0] + bias_ref[...]
    neg_inf = jnp.float32(-jnp.inf)

    for h in range(HEADS):
        sl = slice(h * HEAD_DIM, (h + 1) * HEAD_DIM)
        q = q_ref[0, :, sl]
        k = k_ref[0, :, sl] * (HEAD_DIM ** -0.5)
        v = v_ref[0, :, sl]
        qb, kb, vb = q.astype(BF16), k.astype(BF16), v.astype(BF16)
        ig_c = pre[:, h:h + 1]
        lf_c = _log_sigmoid(pre[:, HEADS + h:HEADS + h + 1])
        ig_r = jnp.sum(jnp.where(eye, ig_c, 0.0), axis=0, keepdims=True)
        lf_r = jnp.sum(jnp.where(eye, lf_c, 0.0), axis=0, keepdims=True)
        bt_c = jnp.sum(jnp.where(tril, lf_r, 0.0), axis=1, keepdims=True)
        bt_r = jnp.sum(jnp.where(row <= col, lf_c, 0.0), axis=0, keepdims=True)
        m_prev = m_scr[h:h + 1, 0:1]
        logw = jnp.where(tril, bt_c - bt_r + ig_r, neg_inf)
        inter = bt_c + m_prev
        m_t = jnp.maximum(inter, jnp.max(logw, axis=1, keepdims=True))
        w = jnp.exp(logw - m_t)
        a_inter = jnp.exp(inter - m_t)
        s = lax.dot_general(qb, kb, (((1,), (1,)), ((), ())), preferred_element_type=F32)
        sw = s * w
        cm = c_scr[h]
        nm = n_scr[h:h + 1, :]
        num = a_inter * jnp.dot(qb, cm.astype(BF16), preferred_element_type=F32) \
            + jnp.dot(sw.astype(BF16), vb, preferred_element_type=F32)
        den = a_inter * jnp.sum(q * nm, axis=1, keepdims=True) + jnp.sum(sw, axis=1, keepdims=True)
        h_ref[0, :, sl] = num / jnp.maximum(jnp.abs(den), jnp.exp(-m_t))
        m_new = m_t[L - 1:L, :]
        bt_last = bt_c[L - 1:L, :]
        g_state = jnp.exp(bt_last + m_prev - m_new)
        w_s = jnp.exp(bt_last - bt_c + ig_c - m_new)
        kw = k * w_s
        c_scr[h] = g_state * cm + lax.dot_general(kw.astype(BF16), vb, (((0,), (0,)), ((), ())),
                                                   preferred_element_type=F32)
        n_scr[h:h + 1, :] = g_state * nm + jnp.sum(kw, axis=0, keepdims=True)
        m_scr[h:h + 1, :] = jnp.broadcast_to(m_new, (1, LANES))

    @pl.when(c == pl.num_programs(1) - 1)
    def _():
        c_out[0] = c_scr[...]
        n_out[0] = n_scr[...]
        m_out[0] = m_scr[...]


def mlstm_scan(z, gz, bias, c0, n0, m0):
    b, t, _ = z.shape
    nc = t // CHUNK
    blk = lambda j: pl.BlockSpec((1, CHUNK, D_MODEL), lambda i, c: (i, c, j))
    st4 = pl.BlockSpec((1, HEADS, HEAD_DIM, HEAD_DIM), lambda i, c: (i, 0, 0, 0))
    st3 = pl.BlockSpec((1, HEADS, HEAD_DIM), lambda i, c: (i, 0, 0))
    stm = pl.BlockSpec((1, HEADS, LANES), lambda i, c: (i, 0, 0))
    m0b = jnp.broadcast_to(m0[:, :, None], (b, HEADS, LANES))
    h, c_new, n_new, m_new = pl.pallas_call(
        _mlstm_kernel, grid=(b, nc),
        in_specs=[blk(0), blk(1), blk(2),
                  pl.BlockSpec((1, CHUNK, LANES), lambda i, c: (i, c, 0)),
                  pl.BlockSpec((1, LANES), lambda i, c: (0, 0)), st4, st3, stm],
        out_specs=[pl.BlockSpec((1, CHUNK, D_MODEL), lambda i, c: (i, c, 0)), st4, st3, stm],
        out_shape=[jax.ShapeDtypeStruct((b, t, D_MODEL), F32),
                   jax.ShapeDtypeStruct((b, HEADS, HEAD_DIM, HEAD_DIM), F32),
                   jax.ShapeDtypeStruct((b, HEADS, HEAD_DIM), F32),
                   jax.ShapeDtypeStruct((b, HEADS, LANES), F32)],
        scratch_shapes=[pltpu.VMEM((HEADS, HEAD_DIM, HEAD_DIM), F32),
                        pltpu.VMEM((HEADS, HEAD_DIM), F32),
                        pltpu.VMEM((HEADS, LANES), F32)],
        compiler_params=_cparams(("parallel", "arbitrary")), name="mlstm_scan",
    )(z, z, z, gz, bias, c0, n0, m0b)
    return h, c_new, n_new, m_new[:, :, 0]


def _mlstm_out_kernel(h_ref, o_ref, x_ref, g_ref, w_ref, out_ref):
    hh = h_ref[...]
    parts = []
    for i in range(HEADS):
        a = hh[:, i * HEAD_DIM:(i + 1) * HEAD_DIM]
        d = a - jnp.mean(a, axis=-1, keepdims=True)
        parts.append(d * lax.rsqrt(jnp.mean(d * d, axis=-1, keepdims=True) + EPS))
    hn = jnp.concatenate(parts, axis=-1) * g_ref[...]
    y = (hn * jax.nn.sigmoid(o_ref[...])).astype(BF16)
    out_ref[...] = x_ref[...] + jnp.dot(y, w_ref[...], preferred_element_type=F32)


def mlstm_out(h, z, x, g_head, w_out, tm=512):
    m, d = x.shape
    tm = min(tm, m)
    row = pl.BlockSpec((tm, d), lambda i: (i, 0))
    return pl.pallas_call(
        _mlstm_out_kernel, grid=(m // tm,),
        in_specs=[row, pl.BlockSpec((tm, d), lambda i: (i, 3)), row,
                  pl.BlockSpec((1, d), lambda i: (0, 0)), pl.BlockSpec((d, d), lambda i: (0, 0))],
        out_specs=row, out_shape=jax.ShapeDtypeStruct((m, d), F32),
        compiler_params=_cparams(("parallel",)), name="mlstm_out",
    )(h, z, x, g_head.reshape(1, d), w_out)


def _pool_kernel(x_ref, pre_ref, g_ref, w_ref, s_ref, out_ref, tail_ref, ext_scr, *, pos0):
    t = pl.program_id(1)
    tb = x_ref.shape[1]
    x = x_ref[0]
    u = _rms(x, g_ref[...])

    @pl.when(t == 0)
    def _():
        ext_scr[0:POOL_HALO, :] = pre_ref[0]

    ext_scr[POOL_HALO:POOL_HALO + tb, :] = u
    pos = pos0 + t * tb + lax.broadcasted_iota(I32, (tb, 1), 0)
    ys = []
    for gi, win in enumerate(POOL_WINDOWS):
        sl = slice(gi * POOL_GROUP, (gi + 1) * POOL_GROUP)
        s = ext_scr[:, sl]
        sh = 1
        while sh < win:
            s = s + pltpu.roll(s, sh, 0)
            sh *= 2
        cnt = jnp.minimum(pos + 1, win).astype(F32)
        p = s[POOL_HALO:, :] / cnt - u[:, sl]
        ys.append(jnp.dot(p.astype(BF16), w_ref[gi], preferred_element_type=F32))
    out_ref[0] = x + jnp.concatenate(ys, axis=-1) * s_ref[...]
    tail = u[tb - POOL_HALO:, :]
    tail_ref[0] = tail
    ext_scr[0:POOL_HALO, :] = tail


def pool_mix(x, prefix, pos0, g, w_pool, s_pool, tb=256):
    b, t, d = x.shape
    tb = min(tb, t)
    return pl.pallas_call(
        functools.partial(_pool_kernel, pos0=pos0), grid=(b, t // tb),
        in_specs=[pl.BlockSpec((1, tb, d), lambda i, j: (i, j, 0)),
                  pl.BlockSpec((1, POOL_HALO, d), lambda i, j: (i, 0, 0)),
                  pl.BlockSpec((1, d), lambda i, j: (0, 0)),
                  pl.BlockSpec(w_pool.shape, lambda i, j: (0, 0, 0)),
                  pl.BlockSpec((1, d), lambda i, j: (0, 0))],
        out_specs=[pl.BlockSpec((1, tb, d), lambda i, j: (i, j, 0)),
                   pl.BlockSpec((1, POOL_HALO, d), lambda i, j: (i, 0, 0))],
        out_shape=[jax.ShapeDtypeStruct((b, t, d), F32), jax.ShapeDtypeStruct((b, POOL_HALO, d), F32)],
        scratch_shapes=[pltpu.VMEM((POOL_HALO + tb, d), F32)],
        compiler_params=_cparams(("parallel", "arbitrary")), name="pool_mix",
    )(x, prefix, g.reshape(1, d), w_pool, s_pool.reshape(1, d))


def _xattn_kernel(x_ref, k_ref, v_ref, g_ref, wq_ref, wo_ref, out_ref):
    x = x_ref[0]
    u = _rms(x, g_ref[...]).astype(BF16)
    q = jnp.dot(u, wq_ref[...], preferred_element_type=F32).astype(BF16)
    outs = []
    for h in range(HEADS):
        sl = slice(h * HEAD_DIM, (h + 1) * HEAD_DIM)
        s = lax.dot_general(q[:, sl], k_ref[0, :, sl], (((1,), (1,)), ((), ())),
                            preferred_element_type=F32) * (HEAD_DIM ** -0.5)
        e = jnp.exp(s - jnp.max(s, axis=-1, keepdims=True))
        p = e / jnp.sum(e, axis=-1, keepdims=True)
        outs.append(jnp.dot(p.astype(BF16), v_ref[0, :, sl], preferred_element_type=F32))
    o = jnp.concatenate(outs, axis=-1).astype(BF16)
    out_ref[0] = x + jnp.dot(o, wo_ref[...], preferred_element_type=F32)


def xattn(x, k, v, g, w_q, w_o, tq=512):
    b, t, d = x.shape
    tq = min(tq, t)
    xs = pl.BlockSpec((1, tq, d), lambda i, j: (i, j, 0))
    kv = pl.BlockSpec((1, N_MEM, d), lambda i, j: (i, 0, 0))
    wsp = pl.BlockSpec((d, d), lambda i, j: (0, 0))
    return pl.pallas_call(
        _xattn_kernel, grid=(b, t // tq),
        in_specs=[xs, kv, kv, pl.BlockSpec((1, d), lambda i, j: (0, 0)), wsp, wsp],
        out_specs=xs, out_shape=jax.ShapeDtypeStruct((b, t, d), F32),
        compiler_params=_cparams(("parallel", "arbitrary")), name="xattn",
    )(x, k, v, g.reshape(1, d), w_q, w_o)


def _top16_rows(work, iota, n_rows, emit):
    neg_inf = jnp.float32(-jnp.inf)

    def body(r, wk):
        mx = jnp.max(wk, axis=0, keepdims=True)
        pos = jnp.min(jnp.where(wk == mx, iota, n_rows), axis=0, keepdims=True)
        emit(r, mx, pos)
        return jnp.where(iota == pos, neg_inf, wk)

    lax.fori_loop(0, PEER_TOPK, body, work)


def _route_kernel(q_ref, keys_ref, eidx_ref, gate_ref, val_scr, idx_scr, top_scr, sel_scr):
    tb = q_ref.shape[0]
    iota_n = lax.broadcasted_iota(I32, (PEER_N_KEYS, tb), 0)
    n_cand = PEER_TOPK * PEER_TOPK
    iota_c = lax.broadcasted_iota(I32, (n_cand, tb), 0)
    for h in range(PEER_HEADS):
        for p in range(2):
            hp = 2 * h + p
            s_t = lax.dot_general(keys_ref[hp], q_ref[:, hp * PEER_HALF:(hp + 1) * PEER_HALF],
                                  (((1,), (1,)), ((), ())), preferred_element_type=F32)

            def emit(r, mx, pos, p=p):
                val_scr[p, pl.ds(r, 1), :] = mx
                idx_scr[p, pl.ds(r, 1), :] = pos

            _top16_rows(s_t, iota_n, PEER_N_KEYS, emit)
        sv0, sv1 = val_scr[0], val_scr[1]
        si0, si1 = idx_scr[0], idx_scr[1]
        cand = jnp.concatenate([sv0[a:a + 1, :] + sv1 for a in range(PEER_TOPK)], axis=0)
        cidx = jnp.concatenate([si0[a:a + 1, :] * PEER_N_KEYS + si1 for a in range(PEER_TOPK)], axis=0)

        def emit2(r, mx, pos):
            top_scr[pl.ds(r, 1), :] = mx
            sel_scr[pl.ds(r, 1), :] = jnp.max(jnp.where(iota_c == pos, cidx, -1), axis=0, keepdims=True)

        _top16_rows(cand, iota_c, n_cand, emit2)
        top = top_scr[...]
        e = jnp.exp(top - jnp.max(top, axis=0, keepdims=True))
        gate_ref[h * PEER_TOPK:(h + 1) * PEER_TOPK, :] = e / jnp.sum(e, axis=0, keepdims=True)
        eidx_ref[h * PEER_TOPK:(h + 1) * PEER_TOPK, :] = sel_scr[...]


def peer_route(q, keys, tb=128):
    m = q.shape[0]
    out = pl.BlockSpec((PEER_SLOTS, tb), lambda i: (0, i))
    return pl.pallas_call(
        _route_kernel, grid=(m // tb,),
        in_specs=[pl.BlockSpec((tb, q.shape[1]), lambda i: (i, 0)),
                  pl.BlockSpec(keys.shape, lambda i: (0, 0, 0))],
        out_specs=[out, out],
        out_shape=[jax.ShapeDtypeStruct((PEER_SLOTS, m), I32), jax.ShapeDtypeStruct((PEER_SLOTS, m), F32)],
        scratch_shapes=[pltpu.VMEM((2, PEER_TOPK, tb), F32), pltpu.VMEM((2, PEER_TOPK, tb), I32),
                        pltpu.VMEM((PEER_TOPK, tb), F32), pltpu.VMEM((PEER_TOPK, tb), I32)],
        compiler_params=_cparams(("parallel",)), name="peer_route",
    )(q, keys)


GROUP = 8
_HI_MASK = 0xFFFF0000


def _unpack(w):
    lo = pltpu.bitcast(w << 16, F32)
    hi = pltpu.bitcast(w & jnp.uint32(_HI_MASK), F32)
    return lo, hi


def _expert_in_kernel(idx_ref, xg_ref, gate_ref, tab_ref, w_ref):
    tbu = gate_ref.shape[0]
    sub = lax.broadcasted_iota(I32, (SUBLANES, LANES), 0)
    lane = lax.broadcasted_iota(I32, (SUBLANES, LANES), 1)
    m2 = (sub % 4) < 2
    m1 = (sub % 2) < 1
    pair_tokens = ((0, 4), (2, 6), (1, 5), (3, 7))

    def group(g, carry):
        base = pl.multiple_of(g * (GROUP * SUBLANES), GROUP * SUBLANES)
        xs = [xg_ref[pl.ds(base + j * SUBLANES, SUBLANES), :] for j in range(GROUP)]

        def slot(k, acc):
            f = []
            for j, (ta, tb_) in enumerate(pair_tokens):
                ia = pl.multiple_of(idx_ref[g * GROUP + ta, k], ROW_SUBLANES)
                ib = pl.multiple_of(idx_ref[g * GROUP + tb_, k], ROW_SUBLANES)
                w = jnp.concatenate([tab_ref[pl.ds(ia, ROW_SUBLANES), :],
                                     tab_ref[pl.ds(ib, ROW_SUBLANES), :]], axis=0)
                lo, hi = _unpack(w)
                f.append(lo * xs[2 * j] + hi * xs[2 * j + 1])
            gs = []
            for a, b in ((f[0], f[1]), (f[2], f[3])):
                gs.append(jnp.where(m2, a, pltpu.roll(b, 2, 0)) + jnp.where(m2, pltpu.roll(a, 6, 0), b))
            hsum = jnp.where(m1, gs[0], pltpu.roll(gs[1], 1, 0)) \
                + jnp.where(m1, pltpu.roll(gs[0], 7, 0), gs[1])
            tot = jnp.sum(hsum, axis=1, keepdims=True)
            return jnp.where(lane == k, tot, acc)

        act = lax.fori_loop(0, PEER_SLOTS, slot, jnp.zeros((SUBLANES, LANES), F32))
        rows = pl.ds(pl.multiple_of(g * GROUP, GROUP), GROUP)
        gelu = 0.5 * act * (1.0 + lax.erf(act * (1.0 / math.sqrt(2.0))))
        w_ref[rows, :] = gate_ref[rows, :] * gelu
        return carry

    lax.fori_loop(0, tbu // GROUP, group, 0)


def expert_in(idx4, xg, gate, tab, tbu=64):
    m = gate.shape[0]
    tbu = min(tbu, m)
    return pl.pallas_call(
        _expert_in_kernel, grid=(m // tbu,),
        in_specs=[pl.BlockSpec((tbu, PEER_SLOTS), lambda i: (i, 0), memory_space=pltpu.SMEM),
                  pl.BlockSpec((tbu * SUBLANES, LANES), lambda i: (i, 0)),
                  pl.BlockSpec((tbu, PEER_SLOTS), lambda i: (i, 0)),
                  pl.BlockSpec(tab.shape, lambda i: (0, 0), pipeline_mode=pl.Buffered(1))],
        out_specs=pl.BlockSpec((tbu, PEER_SLOTS), lambda i: (i, 0)),
        out_shape=jax.ShapeDtypeStruct((m, PEER_SLOTS), F32),
        compiler_params=_cparams(("arbitrary",), VMEM_LIMIT), name="expert_in",
    )(idx4, xg, gate, tab)


def _expert_out_kernel(idx_ref, w_ref, x8_ref, tab_ref, out_ref):
    tbv = idx_ref.shape[0]
    n_acc = 4

    def token(t, carry):
        lo = [jnp.zeros((ROW_SUBLANES, LANES), F32) for _ in range(n_acc)]
        hi = [jnp.zeros((ROW_SUBLANES, LANES), F32) for _ in range(n_acc)]
        for k in range(PEER_SLOTS):
            i = pl.multiple_of(idx_ref[t, k], ROW_SUBLANES)
            a, b = _unpack(tab_ref[pl.ds(i, ROW_SUBLANES), :])
            wk = w_ref[t, k]
            lo[k % n_acc] = lo[k % n_acc] + wk * a
            hi[k % n_acc] = hi[k % n_acc] + wk * b
        lo_s = (lo[0] + lo[1]) + (lo[2] + lo[3])
        hi_s = (hi[0] + hi[1]) + (hi[2] + hi[3])
        rows = pl.ds(pl.multiple_of(t * SUBLANES, SUBLANES), SUBLANES)
        out_ref[rows, :] = x8_ref[rows, :] + jnp.concatenate([lo_s, hi_s], axis=0)
        return carry

    lax.fori_loop(0, tbv, token, 0)


def expert_out(idx4, w, x8, tab, tbv=64):
    m = w.shape[0]
    tbv = min(tbv, m)
    smem = lambda: pl.BlockSpec((tbv, PEER_SLOTS), lambda i: (i, 0), memory_space=pltpu.SMEM)
    rows = pl.BlockSpec((tbv * SUBLANES, LANES), lambda i: (i, 0))
    return pl.pallas_call(
        _expert_out_kernel, grid=(m // tbv,),
        in_specs=[smem(), smem(), rows,
                  pl.BlockSpec(tab.shape, lambda i: (0, 0), pipeline_mode=pl.Buffered(1))],
        out_specs=rows, out_shape=jax.ShapeDtypeStruct(x8.shape, F32),
        compiler_params=_cparams(("arbitrary",), VMEM_LIMIT), name="expert_out",
    )(idx4, w, x8, tab)


def _pack_table(t):
    bits = lax.bitcast_convert_type(t.astype(BF16), jnp.uint16).astype(U32)
    packed = bits[:, :ROW_WORDS] | (bits[:, ROW_WORDS:] << 16)
    return packed.reshape(t.shape[0] * ROW_SUBLANES, LANES)


def _group_tokens(u):
    m = u.shape[0]
    a = u.reshape(m // GROUP, GROUP, 2, ROW_SUBLANES, LANES)
    a = a[:, jnp.array([0, 4, 2, 6, 1, 5, 3, 7])]
    a = a.reshape(m // GROUP, 4, 2, 2, ROW_SUBLANES, LANES)
    a = a.transpose(0, 1, 3, 2, 4, 5)
    return a.reshape(m * SUBLANES, LANES)


def peer(x, g, w_q, keys, tab_u, tab_v):
    m = x.shape[0]
    q, u = norm_matmul(x, g, [w_q], [BF16], emit_u=True)
    eidx_t, gate_t = peer_route(q, keys, tb=min(128, m))
    idx4 = eidx_t.T * ROW_SUBLANES
    w = expert_in(idx4, _group_tokens(u), gate_t.T, tab_u)
    out8 = expert_out(idx4, w, x.reshape(m * SUBLANES, LANES), tab_v)
    return out8.reshape(m, D_MODEL)


def kernel(x_prompt, x_sample, state_mlstm_C, state_mlstm_n, state_mlstm_m, state_pool, cache_mem_k, cache_mem_v, mem_prompt, g_mix, w_mlstm_in, b_mlstm_i, b_mlstm_f, g_mlstm_head, w_mlstm_out, w_pool, s_pool, g_xattn, g_mem, w_xq, w_xk, w_xv, w_xo, g_ffn, w_peer_q, peer_keys, peer_u, peer_v, g_final):
    d = D_MODEL
    bp, tp, _ = x_prompt.shape
    bs, ts, _ = x_sample.shape
    depth = g_mix.shape[0]
    groups = [dict(x=x_prompt.reshape(bp * tp, d), b=bp, t=tp),
              dict(x=x_sample.reshape(bs * ts, d), b=bs, t=ts)]
    outs = [dict(C=[], n=[], m=[], pool=[]) for _ in groups]
    mk_p, mv_p = [], []
    mem = mem_prompt.reshape(bp * N_MEM, d)

    for i in range(depth):
        j = i // 2
        if i % 2 == 0:
            w_in = w_mlstm_in[j]
            w_main = w_in[:, :4 * d].astype(BF16)
            w_gate = jnp.pad(w_in[:, 4 * d:], ((0, 0), (0, LANES - 2 * HEADS)))
            bias = jnp.pad(jnp.concatenate([b_mlstm_i[j], b_mlstm_f[j]]), (0, LANES - 2 * HEADS)).reshape(1, LANES)
            w_out = w_mlstm_out[j].astype(BF16)
        else:
            w_pl = w_pool[j].astype(BF16)
        w_kv = [w_xk[i].astype(BF16), w_xv[i].astype(BF16)]
        w_q, w_o = w_xq[i].astype(BF16), w_xo[i].astype(BF16)
        w_pq = w_peer_q[i].astype(BF16)
        keys = peer_keys[i].reshape(2 * PEER_HEADS, PEER_N_KEYS, PEER_HALF).astype(BF16)
        tab_u, tab_v = _pack_table(peer_u[i]), _pack_table(peer_v[i])

        kp, vp = norm_matmul(mem, g_mem[i], w_kv, [F32, F32])
        mk_p.append(kp.reshape(bp, N_MEM, HEADS, HEAD_DIM))
        mv_p.append(vp.reshape(bp, N_MEM, HEADS, HEAD_DIM))
        kvs = [(kp.reshape(bp, N_MEM, d).astype(BF16), vp.reshape(bp, N_MEM, d).astype(BF16)),
               (cache_mem_k[i].reshape(bs, N_MEM, d).astype(BF16), cache_mem_v[i].reshape(bs, N_MEM, d).astype(BF16))]

        for gi, (grp, out) in enumerate(zip(groups, outs)):
            x, b, t = grp["x"], grp["b"], grp["t"]
            if i % 2 == 0:
                z, gz = norm_matmul(x, g_mix[i], [w_main, w_gate], [F32, F32], hi=(False, True))
                if gi == 0:
                    c0 = jnp.zeros((b, HEADS, HEAD_DIM, HEAD_DIM), F32)
                    n0 = jnp.zeros((b, HEADS, HEAD_DIM), F32)
                    m0 = jnp.zeros((b, HEADS), F32)
                else:
                    c0, n0, m0 = state_mlstm_C[j], state_mlstm_n[j], state_mlstm_m[j]
                h, c1, n1, m1 = mlstm_scan(z.reshape(b, t, 4 * d), gz.reshape(b, t, LANES), bias, c0, n0, m0)
                out["C"].append(c1); out["n"].append(n1); out["m"].append(m1)
                x = mlstm_out(h.reshape(b * t, d), z, x, g_mlstm_head[j], w_out)
            else:
                if gi == 0:
                    prefix, pos0 = jnp.zeros((b, POOL_HALO, d), F32), 0
                else:
                    prefix, pos0 = jnp.pad(state_pool[j], ((0, 0), (1, 0), (0, 0))), 4096
                x3, tail = pool_mix(x.reshape(b, t, d), prefix, pos0, g_mix[i], w_pl, s_pool[j])
                out["pool"].append(tail[:, 1:])
                x = x3.reshape(b * t, d)
            kb, vb = kvs[gi]
            x = xattn(x.reshape(b, t, d), kb, vb, g_xattn[i], w_q, w_o).reshape(b * t, d)
            x = peer(x, g_ffn[i], w_pq, keys, tab_u, tab_v)
            grp["x"] = x

    y = [rmsnorm(grp["x"], g_final).reshape(grp["b"], grp["t"], d) for grp in groups]
    st = lambda lst: jnp.stack(lst)
    po, so = outs
    return (y[0], y[1],
            st(po["C"]), st(po["n"]), st(po["m"]), st(po["pool"]), st(mk_p), st(mv_p),
            st(so["C"]), st(so["n"]), st(so["m"]), st(so["pool"]))
```

```python
import functools
import math

import jax
import jax.numpy as jnp
from jax import lax
from jax.experimental import pallas as pl
from jax.experimental.pallas import tpu as pltpu

F32 = jnp.float32
BF16 = jnp.bfloat16
I32 = jnp.int32
U32 = jnp.uint32
EPS = 1e-6
HIGHEST = lax.Precision.HIGHEST

D_MODEL = 1024
CHUNK = 64
HEADS = 4
HEAD_DIM = D_MODEL // HEADS
POOL_WINDOWS = (2, 4, 8, 16)
POOL_GROUP = D_MODEL // len(POOL_WINDOWS)
POOL_HALO = 16
N_MEM = 256
PEER_HEADS = 8
PEER_N_KEYS = 128
PEER_HALF = 128
PEER_TOPK = 16
PEER_SLOTS = PEER_HEADS * PEER_TOPK
LANES = 128
SUBLANES = 8
ROW_WORDS = D_MODEL // 2
ROW_SUBLANES = ROW_WORDS // LANES
VMEM_LIMIT = 56 * 1024 * 1024


def _cparams(sem, vmem=None):
    return pltpu.CompilerParams(dimension_semantics=sem, vmem_limit_bytes=vmem)


def _rms(x, g):
    return x * lax.rsqrt(jnp.mean(x * x, axis=-1, keepdims=True) + EPS) * g


def _norm_matmul_kernel(*refs, n_w, hi, emit_u, n_chunk):
    x_ref, g_ref = refs[0], refs[1]
    w_refs = refs[2:2 + n_w]
    o_refs = refs[2 + n_w:]
    u = _rms(x_ref[...], g_ref[...])
    ub = u.astype(BF16)
    for w_ref, o_ref, h in zip(w_refs, o_refs, hi):
        n = w_ref.shape[1]
        step = min(n, n_chunk)
        for j in range(0, n, step):
            if h:
                o = jnp.dot(u, w_ref[:, j:j + step], precision=HIGHEST, preferred_element_type=F32)
            else:
                o = jnp.dot(ub, w_ref[:, j:j + step], preferred_element_type=F32)
            o_ref[:, j:j + step] = o.astype(o_ref.dtype)
    if emit_u:
        o_refs[n_w][...] = u


def norm_matmul(x, g, ws, out_dtypes, hi=None, emit_u=False, tm=512):
    m, d = x.shape
    tm = min(tm, m)
    hi = tuple(hi or (False,) * len(ws))
    in_specs = [pl.BlockSpec((tm, d), lambda i: (i, 0)), pl.BlockSpec((1, d), lambda i: (0, 0))]
    in_specs += [pl.BlockSpec(w.shape, lambda i: (0, 0)) for w in ws]
    out_shape = [jax.ShapeDtypeStruct((m, w.shape[1]), dt) for w, dt in zip(ws, out_dtypes)]
    out_specs = [pl.BlockSpec((tm, w.shape[1]), lambda i: (i, 0)) for w in ws]
    if emit_u:
        out_shape.append(jax.ShapeDtypeStruct((m, d), F32))
        out_specs.append(pl.BlockSpec((tm, d), lambda i: (i, 0)))
    return pl.pallas_call(
        functools.partial(_norm_matmul_kernel, n_w=len(ws), hi=hi, emit_u=emit_u, n_chunk=512),
        grid=(m // tm,), in_specs=in_specs, out_specs=out_specs, out_shape=out_shape,
        compiler_params=_cparams(("parallel",), VMEM_LIMIT), name="norm_matmul",
    )(x, g.reshape(1, d), *ws)


def _rmsnorm_kernel(x_ref, g_ref, o_ref):
    o_ref[...] = _rms(x_ref[...], g_ref[...])


def rmsnorm(x, g, tm=512):
    m, d = x.shape
    tm = min(tm, m)
    return pl.pallas_call(
        _rmsnorm_kernel, grid=(m // tm,),
        in_specs=[pl.BlockSpec((tm, d), lambda i: (i, 0)), pl.BlockSpec((1, d), lambda i: (0, 0))],
        out_specs=pl.BlockSpec((tm, d), lambda i: (i, 0)),
        out_shape=jax.ShapeDtypeStruct((m, d), F32),
        compiler_params=_cparams(("parallel",)), name="final_rmsnorm",
    )(x, g.reshape(1, d))


def _log_sigmoid(x):
    return jnp.minimum(x, 0.0) - jnp.log1p(jnp.exp(-jnp.abs(x)))


def _mlstm_kernel(q_ref, k_ref, v_ref, gz_ref, bias_ref, c0_ref, n0_ref, m0_ref,
                  h_ref, c_out, n_out, m_out, c_scr, n_scr, m_scr):
    c = pl.program_id(1)
    L = q_ref.shape[1]

    @pl.when(c == 0)
    def _():
        c_scr[...] = c0_ref[0]
        n_scr[...] = n0_ref[0]
        m_scr[...] = m0_ref[0]

    row = lax.broadcasted_iota(I32, (L, L), 0)
    col = lax.broadcasted_iota(I32, (L, L), 1)
    tril = col <= row
    eye = col == row
    pre = gz_ref[0] + bias_ref[...]
    neg_inf = jnp.float32(-jnp.inf)

    for h in range(HEADS):
        sl = slice(h * HEAD_DIM, (h + 1) * HEAD_DIM)
        q = q_ref[0, :, sl]
        k = k_ref[0, :, sl] * (HEAD_DIM ** -0.5)
        v = v_ref[0, :, sl]
        qb, kb, vb = q.astype(BF16), k.astype(BF16), v.astype(BF16)
        ig_c = pre[:, h:h + 1]
        lf_c = _log_sigmoid(pre[:, HEADS + h:HEADS + h + 1])
        ig_r = jnp.sum(jnp.where(eye, ig_c, 0.0), axis=0, keepdims=True)
        lf_r = jnp.sum(jnp.where(eye, lf_c, 0.0), axis=0, keepdims=True)
        bt_c = jnp.sum(jnp.where(tril, lf_r, 0.0), axis=1, keepdims=True)
        bt_r = jnp.sum(jnp.where(row <= col, lf_c, 0.0), axis=0, keepdims=True)
        m_prev = m_scr[h:h + 1, 0:1]
        logw = jnp.where(tril, bt_c - bt_r + ig_r, neg_inf)
        inter = bt_c + m_prev
        m_t = jnp.maximum(inter, jnp.max(logw, axis=1, keepdims=True))
        w = jnp.exp(logw - m_t)
        a_inter = jnp.exp(inter - m_t)
        s = lax.dot_general(qb, kb, (((1,), (1,)), ((), ())), preferred_element_type=F32)
        sw = s * w
        cm = c_scr[h]
        nm = n_scr[h:h + 1, :]
        num = a_inter * jnp.dot(qb, cm.astype(BF16), preferred_element_type=F32) \
            + jnp.dot(sw.astype(BF16), vb, preferred_element_type=F32)
        den = a_inter * jnp.sum(q * nm, axis=1, keepdims=True) + jnp.sum(sw, axis=1, keepdims=True)
        h_ref[0, :, sl] = num / jnp.maximum(jnp.abs(den), jnp.exp(-m_t))
        m_new = m_t[L - 1:L, :]
        bt_last = bt_c[L - 1:L, :]
        g_state = jnp.exp(bt_last + m_prev - m_new)
        w_s = jnp.exp(bt_last - bt_c + ig_c - m_new)
        kw = k * w_s
        c_scr[h] = g_state * cm + lax.dot_general(kw.astype(BF16), vb, (((0,), (0,)), ((), ())),
                                                   preferred_element_type=F32)
        n_scr[h:h + 1, :] = g_state * nm + jnp.sum(kw, axis=0, keepdims=True)
        m_scr[h:h + 1, :] = jnp.broadcast_to(m_new, (1, LANES))

    @pl.when(c == pl.num_programs(1) - 1)
    def _():
        c_out[0] = c_scr[...]
        n_out[0] = n_scr[...]
        m_out[0] = m_scr[...]


def mlstm_scan(z, gz, bias, c0, n0, m0):
    b, t, _ = z.shape
    nc = t // CHUNK
    blk = lambda j: pl.BlockSpec((1, CHUNK, D_MODEL), lambda i, c: (i, c, j))
    st4 = pl.BlockSpec((1, HEADS, HEAD_DIM, HEAD_DIM), lambda i, c: (i, 0, 0, 0))
    st3 = pl.BlockSpec((1, HEADS, HEAD_DIM), lambda i, c: (i, 0, 0))
    stm = pl.BlockSpec((1, HEADS, LANES), lambda i, c: (i, 0, 0))
    m0b = jnp.broadcast_to(m0[:, :, None], (b, HEADS, LANES))
    h, c_new, n_new, m_new = pl.pallas_call(
        _mlstm_kernel, grid=(b, nc),
        in_specs=[blk(0), blk(1), blk(2),
                  pl.BlockSpec((1, CHUNK, LANES), lambda i, c: (i, c, 0)),
                  pl.BlockSpec((1, LANES), lambda i, c: (0, 0)), st4, st3, stm],
        out_specs=[pl.BlockSpec((1, CHUNK, D_MODEL), lambda i, c: (i, c, 0)), st4, st3, stm],
        out_shape=[jax.ShapeDtypeStruct((b, t, D_MODEL), F32),
                   jax.ShapeDtypeStruct((b, HEADS, HEAD_DIM, HEAD_DIM), F32),
                   jax.ShapeDtypeStruct((b, HEADS, HEAD_DIM), F32),
                   jax.ShapeDtypeStruct((b, HEADS, LANES), F32)],
        scratch_shapes=[pltpu.VMEM((HEADS, HEAD_DIM, HEAD_DIM), F32),
                        pltpu.VMEM((HEADS, HEAD_DIM), F32),
                        pltpu.VMEM((HEADS, LANES), F32)],
        compiler_params=_cparams(("parallel", "arbitrary")), name="mlstm_scan",
    )(z, z, z, gz, bias, c0, n0, m0b)
    return h, c_new, n_new, m_new[:, :, 0]


def _mlstm_out_kernel(h_ref, o_ref, x_ref, g_ref, w_ref, out_ref):
    hh = h_ref[...]
    parts = []
    for i in range(HEADS):
        a = hh[:, i * HEAD_DIM:(i + 1) * HEAD_DIM]
        d = a - jnp.mean(a, axis=-1, keepdims=True)
        parts.append(d * lax.rsqrt(jnp.mean(d * d, axis=-1, keepdims=True) + EPS))
    hn = jnp.concatenate(parts, axis=-1) * g_ref[...]
    y = (hn * jax.nn.sigmoid(o_ref[...])).astype(BF16)
    out_ref[...] = x_ref[...] + jnp.dot(y, w_ref[...], preferred_element_type=F32)


def mlstm_out(h, z, x, g_head, w_out, tm=512):
    m, d = x.shape
    tm = min(tm, m)
    row = pl.BlockSpec((tm, d), lambda i: (i, 0))
    return pl.pallas_call(
        _mlstm_out_kernel, grid=(m // tm,),
        in_specs=[row, pl.BlockSpec((tm, d), lambda i: (i, 3)), row,
                  pl.BlockSpec((1, d), lambda i: (0, 0)), pl.BlockSpec((d, d), lambda i: (0, 0))],
        out_specs=row, out_shape=jax.ShapeDtypeStruct((m, d), F32),
        compiler_params=_cparams(("parallel",)), name="mlstm_out",
    )(h, z, x, g_head.reshape(1, d), w_out)


def _pool_kernel(x_ref, pre_ref, g_ref, w_ref, s_ref, out_ref, tail_ref, ext_scr, *, pos0):
    t = pl.program_id(1)
    tb = x_ref.shape[1]
    x = x_ref[0]
    u = _rms(x, g_ref[...])

    @pl.when(t == 0)
    def _():
        ext_scr[0:POOL_HALO, :] = pre_ref[0]

    ext_scr[POOL_HALO:POOL_HALO + tb, :] = u
    pos = pos0 + t * tb + lax.broadcasted_iota(I32, (tb, 1), 0)
    ys = []
    for gi, win in enumerate(POOL_WINDOWS):
        sl = slice(gi * POOL_GROUP, (gi + 1) * POOL_GROUP)
        s = ext_scr[:, sl]
        sh = 1
        while sh < win:
            s = s + pltpu.roll(s, sh, 0)
            sh *= 2
        cnt = jnp.minimum(pos + 1, win).astype(F32)
        p = s[POOL_HALO:, :] / cnt - u[:, sl]
        ys.append(jnp.dot(p.astype(BF16), w_ref[gi], preferred_element_type=F32))
    out_ref[0] = x + jnp.concatenate(ys, axis=-1) * s_ref[...]
    tail = u[tb - POOL_HALO:, :]
    tail_ref[0] = tail
    ext_scr[0:POOL_HALO, :] = tail


def pool_mix(x, prefix, pos0, g, w_pool, s_pool, tb=256):
    b, t, d = x.shape
    tb = min(tb, t)
    return pl.pallas_call(
        functools.partial(_pool_kernel, pos0=pos0), grid=(b, t // tb),
        in_specs=[pl.BlockSpec((1, tb, d), lambda i, j: (i, j, 0)),
                  pl.BlockSpec((1, POOL_HALO, d), lambda i, j: (i, 0, 0)),
                  pl.BlockSpec((1, d), lambda i, j: (0, 0)),
                  pl.BlockSpec(w_pool.shape, lambda i, j: (0, 0, 0)),
                  pl.BlockSpec((1, d), lambda i, j: (0, 0))],
        out_specs=[pl.BlockSpec((1, tb, d), lambda i, j: (i, j, 0)),
                   pl.BlockSpec((1, POOL_HALO, d), lambda i, j: (i, 0, 0))],
        out_shape=[jax.ShapeDtypeStruct((b, t, d), F32), jax.ShapeDtypeStruct((b, POOL_HALO, d), F32)],
        scratch_shapes=[pltpu.VMEM((POOL_HALO + tb, d), F32)],
        compiler_params=_cparams(("parallel", "arbitrary")), name="pool_mix",
    )(x, prefix, g.reshape(1, d), w_pool, s_pool.reshape(1, d))


def _xattn_kernel(x_ref, k_ref, v_ref, g_ref, wq_ref, wo_ref, out_ref):
    x = x_ref[0]
    u = _rms(x, g_ref[...]).astype(BF16)
    q = jnp.dot(u, wq_ref[...], preferred_element_type=F32).astype(BF16)
    outs = []
    for h in range(HEADS):
        sl = slice(h * HEAD_DIM, (h + 1) * HEAD_DIM)
        s = lax.dot_general(q[:, sl], k_ref[0, :, sl], (((1,), (1,)), ((), ())),
                            preferred_element_type=F32) * (HEAD_DIM ** -0.5)
        e = jnp.exp(s - jnp.max(s, axis=-1, keepdims=True))
        p = e / jnp.sum(e, axis=-1, keepdims=True)
        outs.append(jnp.dot(p.astype(BF16), v_ref[0, :, sl], preferred_element_type=F32))
    o = jnp.concatenate(outs, axis=-1).astype(BF16)
    out_ref[0] = x + jnp.dot(o, wo_ref[...], preferred_element_type=F32)


def xattn(x, k, v, g, w_q, w_o, tq=512):
    b, t, d = x.shape
    tq = min(tq, t)
    xs = pl.BlockSpec((1, tq, d), lambda i, j: (i, j, 0))
    kv = pl.BlockSpec((1, N_MEM, d), lambda i, j: (i, 0, 0))
    wsp = pl.BlockSpec((d, d), lambda i, j: (0, 0))
    return pl.pallas_call(
        _xattn_kernel, grid=(b, t // tq),
        in_specs=[xs, kv, kv, pl.BlockSpec((1, d), lambda i, j: (0, 0)), wsp, wsp],
        out_specs=xs, out_shape=jax.ShapeDtypeStruct((b, t, d), F32),
        compiler_params=_cparams(("parallel", "arbitrary")), name="xattn",
    )(x, k, v, g.reshape(1, d), w_q, w_o)


def _top16_rows(works, iota, emits):
    neg_inf = jnp.float32(-jnp.inf)
    n_rows = iota.shape[0]

    def body(r, wks):
        out = []
        for wk, emit in zip(wks, emits):
            mx = jnp.max(wk, axis=0, keepdims=True)
            pos = jnp.min(jnp.where(wk == mx, iota, n_rows), axis=0, keepdims=True)
            emit(r, mx, pos)
            out.append(jnp.where(iota == pos, neg_inf, wk))
        return tuple(out)

    lax.fori_loop(0, PEER_TOPK, body, tuple(works))


_N_CAND_ROWS = PEER_TOPK + 7 * SUBLANES + SUBLANES


def _route_kernel(q_ref, keys_ref, eidx_ref, gate_ref, val_scr, idx_scr, top_scr, sel_scr):
    tb = q_ref.shape[0]
    iota_n = lax.broadcasted_iota(I32, (PEER_N_KEYS, tb), 0)
    iota_c = lax.broadcasted_iota(I32, (_N_CAND_ROWS, tb), 0)
    row8 = lax.broadcasted_iota(I32, (SUBLANES, tb), 0)
    neg_inf = jnp.float32(-jnp.inf)

    def scores(hp):
        return lax.dot_general(keys_ref[hp], q_ref[:, hp * PEER_HALF:(hp + 1) * PEER_HALF],
                               (((1,), (1,)), ((), ())), preferred_element_type=F32)

    def emit1(j):
        def emit(r, mx, pos):
            val_scr[j, pl.ds(r, 1), :] = mx
            idx_scr[j, pl.ds(r, 1), :] = pos
        return emit

    def candidates(j):
        sv0, sv1, si0, si1 = val_scr[j], val_scr[j + 1], idx_scr[j], idx_scr[j + 1]
        vals = [sv0[0:1, :] + sv1]
        ids = [si0[0:1, :] * PEER_N_KEYS + si1]
        for a in range(1, SUBLANES):
            keep = row8 < PEER_TOPK // (a + 1)
            vals.append(jnp.where(keep, sv0[a:a + 1, :] + sv1[0:SUBLANES, :], neg_inf))
            ids.append(si0[a:a + 1, :] * PEER_N_KEYS + si1[0:SUBLANES, :])
        vals.append(sv0[SUBLANES:, :] + sv1[0:1, :])
        ids.append(si0[SUBLANES:, :] * PEER_N_KEYS + si1[0:1, :])
        return jnp.concatenate(vals, axis=0), jnp.concatenate(ids, axis=0)

    def emit2(j, cidx):
        def emit(r, mx, pos):
            top_scr[j, pl.ds(r, 1), :] = mx
            sel_scr[j, pl.ds(r, 1), :] = jnp.max(jnp.where(iota_c == pos, cidx, -1), axis=0, keepdims=True)
        return emit

    for h0 in range(0, PEER_HEADS, 2):
        for j in range(2):
            hp = 2 * (h0 + j)
            _top16_rows([scores(hp), scores(hp + 1)], iota_n, [emit1(2 * j), emit1(2 * j + 1)])
        (cand_a, cidx_a), (cand_b, cidx_b) = candidates(0), candidates(2)
        _top16_rows([cand_a, cand_b], iota_c, [emit2(0, cidx_a), emit2(1, cidx_b)])
        for j in range(2):
            top = top_scr[j]
            e = jnp.exp(top - jnp.max(top, axis=0, keepdims=True))
            rows = slice((h0 + j) * PEER_TOPK, (h0 + j + 1) * PEER_TOPK)
            gate_ref[rows, :] = e / jnp.sum(e, axis=0, keepdims=True)
            eidx_ref[rows, :] = sel_scr[j]


def peer_route(q, keys, tb=128):
    m = q.shape[0]
    out = pl.BlockSpec((PEER_SLOTS, tb), lambda i: (0, i))
    return pl.pallas_call(
        _route_kernel, grid=(m // tb,),
        in_specs=[pl.BlockSpec((tb, q.shape[1]), lambda i: (i, 0)),
                  pl.BlockSpec(keys.shape, lambda i: (0, 0, 0))],
        out_specs=[out, out],
        out_shape=[jax.ShapeDtypeStruct((PEER_SLOTS, m), I32), jax.ShapeDtypeStruct((PEER_SLOTS, m), F32)],
        scratch_shapes=[pltpu.VMEM((4, PEER_TOPK, tb), F32), pltpu.VMEM((4, PEER_TOPK, tb), I32),
                        pltpu.VMEM((2, PEER_TOPK, tb), F32), pltpu.VMEM((2, PEER_TOPK, tb), I32)],
        compiler_params=_cparams(("parallel",)), name="peer_route",
    )(q, keys)


GROUP = 8
SLOT_UNROLL = LANES // GROUP
SLOT_TRIPS = PEER_SLOTS // SLOT_UNROLL
_HI_MASK = 0xFFFF0000


def _unpack(w):
    lo = pltpu.bitcast(w << 16, F32)
    hi = pltpu.bitcast(w & jnp.uint32(_HI_MASK), F32)
    return lo, hi


def _expert_in_kernel(idx_ref, xg_ref, gate_ref, tab_ref, w_ref, part_scr):
    tbu = gate_ref.shape[0]
    sub = lax.broadcasted_iota(I32, (SUBLANES, LANES), 0)
    lane = lax.broadcasted_iota(I32, (SUBLANES, LANES), 1)
    m2 = (sub % 4) < 2
    m1 = (sub % 2) < 1
    pair_tokens = ((0, 4), (2, 6), (1, 5), (3, 7))

    def group(g, carry):
        base = pl.multiple_of(g * (GROUP * SUBLANES), GROUP * SUBLANES)
        xs = [xg_ref[pl.ds(base + j * SUBLANES, SUBLANES), :] for j in range(GROUP)]

        def slots(kb, c):
            for kk in range(SLOT_UNROLL):
                k = kb * SLOT_UNROLL + kk
                f = []
                for j, (ta, tb_) in enumerate(pair_tokens):
                    ia = pl.multiple_of(idx_ref[g * SLOT_TRIPS + kb, kk * GROUP + ta], ROW_SUBLANES)
                    ib = pl.multiple_of(idx_ref[g * SLOT_TRIPS + kb, kk * GROUP + tb_], ROW_SUBLANES)
                    w = jnp.concatenate([tab_ref[pl.ds(ia, ROW_SUBLANES), :],
                                         tab_ref[pl.ds(ib, ROW_SUBLANES), :]], axis=0)
                    lo, hi = _unpack(w)
                    f.append(lo * xs[2 * j] + hi * xs[2 * j + 1])
                gs = []
                for a, b in ((f[0], f[1]), (f[2], f[3])):
                    gs.append(jnp.where(m2, a, pltpu.roll(b, 2, 0)) + jnp.where(m2, pltpu.roll(a, 6, 0), b))
                part_scr[pl.ds(pl.multiple_of(k * SUBLANES, SUBLANES), SUBLANES), :] = (
                    jnp.where(m1, gs[0], pltpu.roll(gs[1], 1, 0)) + jnp.where(m1, pltpu.roll(gs[0], 7, 0), gs[1]))
            return c

        lax.fori_loop(0, SLOT_TRIPS, slots, 0)
        accs = [jnp.zeros((SUBLANES, LANES), F32) for _ in range(4)]
        for k in range(PEER_SLOTS):
            tot = jnp.sum(part_scr[k * SUBLANES:(k + 1) * SUBLANES, :], axis=1, keepdims=True)
            accs[k % 4] = jnp.where(lane == k, tot, accs[k % 4])
        act = (accs[0] + accs[1]) + (accs[2] + accs[3])
        rows = pl.ds(pl.multiple_of(g * GROUP, GROUP), GROUP)
        gelu = 0.5 * act * (1.0 + lax.erf(act * (1.0 / math.sqrt(2.0))))
        w_ref[rows, :] = gate_ref[rows, :] * gelu
        return carry

    lax.fori_loop(0, tbu // GROUP, group, 0)


def expert_in(idx_g, xg, gate, tab, tbu=64):
    m = gate.shape[0]
    tbu = min(tbu, m)
    return pl.pallas_call(
        _expert_in_kernel, grid=(m // tbu,),
        in_specs=[pl.BlockSpec((tbu, LANES), lambda i: (i, 0), memory_space=pltpu.SMEM),
                  pl.BlockSpec((tbu * SUBLANES, LANES), lambda i: (i, 0)),
                  pl.BlockSpec((tbu, PEER_SLOTS), lambda i: (i, 0)),
                  pl.BlockSpec(tab.shape, lambda i: (0, 0), pipeline_mode=pl.Buffered(1))],
        out_specs=pl.BlockSpec((tbu, PEER_SLOTS), lambda i: (i, 0)),
        out_shape=jax.ShapeDtypeStruct((m, PEER_SLOTS), F32),
        scratch_shapes=[pltpu.VMEM((PEER_SLOTS * SUBLANES, LANES), F32)],
        compiler_params=_cparams(("arbitrary",), VMEM_LIMIT), name="expert_in",
    )(idx_g, xg, gate, tab)


def _expert_out_kernel(idx_ref, w_ref, x8_ref, tab_ref, out_ref):
    tbv = idx_ref.shape[0]
    n_acc = 4

    def token(t, carry):
        lo = [jnp.zeros((ROW_SUBLANES, LANES), F32) for _ in range(n_acc)]
        hi = [jnp.zeros((ROW_SUBLANES, LANES), F32) for _ in range(n_acc)]
        for k in range(PEER_SLOTS):
            i = pl.multiple_of(idx_ref[t, k], ROW_SUBLANES)
            a, b = _unpack(tab_ref[pl.ds(i, ROW_SUBLANES), :])
            wk = w_ref[t, k]
            lo[k % n_acc] = lo[k % n_acc] + wk * a
            hi[k % n_acc] = hi[k % n_acc] + wk * b
        lo_s = (lo[0] + lo[1]) + (lo[2] + lo[3])
        hi_s = (hi[0] + hi[1]) + (hi[2] + hi[3])
        rows = pl.ds(pl.multiple_of(t * SUBLANES, SUBLANES), SUBLANES)
        out_ref[rows, :] = x8_ref[rows, :] + jnp.concatenate([lo_s, hi_s], axis=0)
        return carry

    lax.fori_loop(0, tbv, token, 0)


def expert_out(idx4, w, x8, tab, tbv=64):
    m = w.shape[0]
    tbv = min(tbv, m)
    smem = lambda: pl.BlockSpec((tbv, PEER_SLOTS), lambda i: (i, 0), memory_space=pltpu.SMEM)
    rows = pl.BlockSpec((tbv * SUBLANES, LANES), lambda i: (i, 0))
    return pl.pallas_call(
        _expert_out_kernel, grid=(m // tbv,),
        in_specs=[smem(), smem(), rows,
                  pl.BlockSpec(tab.shape, lambda i: (0, 0), pipeline_mode=pl.Buffered(1))],
        out_specs=rows, out_shape=jax.ShapeDtypeStruct(x8.shape, F32),
        compiler_params=_cparams(("arbitrary",), VMEM_LIMIT), name="expert_out",
    )(idx4, w, x8, tab)


def _pack_table(t):
    bits = lax.bitcast_convert_type(t.astype(BF16), jnp.uint16).astype(U32)
    packed = bits[:, :ROW_WORDS] | (bits[:, ROW_WORDS:] << 16)
    return packed.reshape(t.shape[0] * ROW_SUBLANES, LANES)


def _group_tokens(u):
    m = u.shape[0]
    a = u.reshape(m // GROUP, GROUP, 2, ROW_SUBLANES, LANES)
    a = a[:, jnp.array([0, 4, 2, 6, 1, 5, 3, 7])]
    a = a.reshape(m // GROUP, 4, 2, 2, ROW_SUBLANES, LANES)
    a = a.transpose(0, 1, 3, 2, 4, 5)
    return a.reshape(m * SUBLANES, LANES)


def peer(x, g, w_q, keys, tab_u, tab_v):
    m = x.shape[0]
    q, u = norm_matmul(x, g, [w_q], [BF16], emit_u=True)
    eidx_t, gate_t = peer_route(q, keys, tb=min(128, m))
    idx4 = eidx_t.T * ROW_SUBLANES
    idx_g = idx4.reshape(m // GROUP, GROUP, PEER_SLOTS).transpose(0, 2, 1).reshape(m, LANES)
    w = expert_in(idx_g, _group_tokens(u), gate_t.T, tab_u)
    out8 = expert_out(idx4, w, x.reshape(m * SUBLANES, LANES), tab_v)
    return out8.reshape(m, D_MODEL)


def kernel(x_prompt, x_sample, state_mlstm_C, state_mlstm_n, state_mlstm_m, state_pool, cache_mem_k, cache_mem_v, mem_prompt, g_mix, w_mlstm_in, b_mlstm_i, b_mlstm_f, g_mlstm_head, w_mlstm_out, w_pool, s_pool, g_xattn, g_mem, w_xq, w_xk, w_xv, w_xo, g_ffn, w_peer_q, peer_keys, peer_u, peer_v, g_final):
    d = D_MODEL
    bp, tp, _ = x_prompt.shape
    bs, ts, _ = x_sample.shape
    depth = g_mix.shape[0]
    groups = [dict(x=x_prompt.reshape(bp * tp, d), b=bp, t=tp),
              dict(x=x_sample.reshape(bs * ts, d), b=bs, t=ts)]
    outs = [dict(C=[], n=[], m=[], pool=[]) for _ in groups]
    mk_p, mv_p = [], []
    mem = mem_prompt.reshape(bp * N_MEM, d)

    for i in range(depth):
        j = i // 2
        if i % 2 == 0:
            w_in = w_mlstm_in[j]
            w_main = w_in[:, :4 * d].astype(BF16)
            w_gate = jnp.pad(w_in[:, 4 * d:], ((0, 0), (0, LANES - 2 * HEADS)))
            bias = jnp.pad(jnp.concatenate([b_mlstm_i[j], b_mlstm_f[j]]), (0, LANES - 2 * HEADS)).reshape(1, LANES)
            w_out = w_mlstm_out[j].astype(BF16)
        else:
            w_pl = w_pool[j].astype(BF16)
        w_kv = [w_xk[i].astype(BF16), w_xv[i].astype(BF16)]
        w_q, w_o = w_xq[i].astype(BF16), w_xo[i].astype(BF16)
        w_pq = w_peer_q[i].astype(BF16)
        keys = peer_keys[i].reshape(2 * PEER_HEADS, PEER_N_KEYS, PEER_HALF).astype(BF16)
        tab_u, tab_v = _pack_table(peer_u[i]), _pack_table(peer_v[i])

        kp, vp = norm_matmul(mem, g_mem[i], w_kv, [F32, F32])
        mk_p.append(kp.reshape(bp, N_MEM, HEADS, HEAD_DIM))
        mv_p.append(vp.reshape(bp, N_MEM, HEADS, HEAD_DIM))
        kvs = [(kp.reshape(bp, N_MEM, d).astype(BF16), vp.reshape(bp, N_MEM, d).astype(BF16)),
               (cache_mem_k[i].reshape(bs, N_MEM, d).astype(BF16), cache_mem_v[i].reshape(bs, N_MEM, d).astype(BF16))]

        for gi, (grp, out) in enumerate(zip(groups, outs)):
            x, b, t = grp["x"], grp["b"], grp["t"]
            if i % 2 == 0:
                z, gz = norm_matmul(x, g_mix[i], [w_main, w_gate], [F32, F32], hi=(False, True))
                if gi == 0:
                    c0 = jnp.zeros((b, HEADS, HEAD_DIM, HEAD_DIM), F32)
                    n0 = jnp.zeros((b, HEADS, HEAD_DIM), F32)
                    m0 = jnp.zeros((b, HEADS), F32)
                else:
                    c0, n0, m0 = state_mlstm_C[j], state_mlstm_n[j], state_mlstm_m[j]
                h, c1, n1, m1 = mlstm_scan(z.reshape(b, t, 4 * d), gz.reshape(b, t, LANES), bias, c0, n0, m0)
                out["C"].append(c1); out["n"].append(n1); out["m"].append(m1)
                x = mlstm_out(h.reshape(b * t, d), z, x, g_mlstm_head[j], w_out)
            else:
                if gi == 0:
                    prefix, pos0 = jnp.zeros((b, POOL_HALO, d), F32), 0
                else:
                    prefix, pos0 = jnp.pad(state_pool[j], ((0, 0), (1, 0), (0, 0))), 4096
                x3, tail = pool_mix(x.reshape(b, t, d), prefix, pos0, g_mix[i], w_pl, s_pool[j])
                out["pool"].append(tail[:, 1:])
                x = x3.reshape(b * t, d)
            kb, vb = kvs[gi]
            x = xattn(x.reshape(b, t, d), kb, vb, g_xattn[i], w_q, w_o).reshape(b * t, d)
            x = peer(x, g_ffn[i], w_pq, keys, tab_u, tab_v)
            grp["x"] = x

    y = [rmsnorm(grp["x"], g_final).reshape(grp["b"], grp["t"], d) for grp in groups]
    st = lambda lst: jnp.stack(lst)
    po, so = outs
    return (y[0], y[1],
            st(po["C"]), st(po["n"]), st(po["m"]), st(po["pool"]), st(mk_p), st(mv_p),
            st(so["C"]), st(so["n"]), st(so["m"]), st(so["pool"]))
```

```python
import functools
import math

import jax
import jax.numpy as jnp
from jax import lax
from jax.experimental import pallas as pl
from jax.experimental.pallas import tpu as pltpu

F32 = jnp.float32
BF16 = jnp.bfloat16
I32 = jnp.int32
U32 = jnp.uint32
EPS = 1e-6
HIGHEST = lax.Precision.HIGHEST

D_MODEL = 1024
CHUNK = 64
HEADS = 4
HEAD_DIM = D_MODEL // HEADS
POOL_WINDOWS = (2, 4, 8, 16)
POOL_GROUP = D_MODEL // len(POOL_WINDOWS)
POOL_HALO = 16
N_MEM = 256
PEER_HEADS = 8
PEER_N_KEYS = 128
PEER_HALF = 128
PEER_TOPK = 16
PEER_SLOTS = PEER_HEADS * PEER_TOPK
LANES = 128
SUBLANES = 8
ROW_WORDS = D_MODEL // 2
ROW_SUBLANES = ROW_WORDS // LANES
VMEM_LIMIT = 56 * 1024 * 1024


def _cparams(sem, vmem=None):
    return pltpu.CompilerParams(dimension_semantics=sem, vmem_limit_bytes=vmem)


def _rms(x, g):
    return x * lax.rsqrt(jnp.mean(x * x, axis=-1, keepdims=True) + EPS) * g


def _norm_matmul_kernel(*refs, n_w, hi, emit_u, n_chunk):
    x_ref, g_ref = refs[0], refs[1]
    w_refs = refs[2:2 + n_w]
    o_refs = refs[2 + n_w:]
    u = _rms(x_ref[...], g_ref[...])
    ub = u.astype(BF16)
    for w_ref, o_ref, h in zip(w_refs, o_refs, hi):
        n = w_ref.shape[1]
        step = min(n, n_chunk)
        for j in range(0, n, step):
            if h:
                o = jnp.dot(u, w_ref[:, j:j + step], precision=HIGHEST, preferred_element_type=F32)
            else:
                o = jnp.dot(ub, w_ref[:, j:j + step], preferred_element_type=F32)
            o_ref[:, j:j + step] = o.astype(o_ref.dtype)
    if emit_u:
        o_refs[n_w][...] = u


def norm_matmul(x, g, ws, out_dtypes, hi=None, emit_u=False, tm=512):
    m, d = x.shape
    tm = min(tm, m)
    hi = tuple(hi or (False,) * len(ws))
    in_specs = [pl.BlockSpec((tm, d), lambda i: (i, 0)), pl.BlockSpec((1, d), lambda i: (0, 0))]
    in_specs += [pl.BlockSpec(w.shape, lambda i: (0, 0)) for w in ws]
    out_shape = [jax.ShapeDtypeStruct((m, w.shape[1]), dt) for w, dt in zip(ws, out_dtypes)]
    out_specs = [pl.BlockSpec((tm, w.shape[1]), lambda i: (i, 0)) for w in ws]
    if emit_u:
        out_shape.append(jax.ShapeDtypeStruct((m, d), F32))
        out_specs.append(pl.BlockSpec((tm, d), lambda i: (i, 0)))
    return pl.pallas_call(
        functools.partial(_norm_matmul_kernel, n_w=len(ws), hi=hi, emit_u=emit_u, n_chunk=512),
        grid=(m // tm,), in_specs=in_specs, out_specs=out_specs, out_shape=out_shape,
        compiler_params=_cparams(("parallel",), VMEM_LIMIT), name="norm_matmul",
    )(x, g.reshape(1, d), *ws)


def _rmsnorm_kernel(x_ref, g_ref, o_ref):
    o_ref[...] = _rms(x_ref[...], g_ref[...])


def rmsnorm(x, g, tm=512):
    m, d = x.shape
    tm = min(tm, m)
    return pl.pallas_call(
        _rmsnorm_kernel, grid=(m // tm,),
        in_specs=[pl.BlockSpec((tm, d), lambda i: (i, 0)), pl.BlockSpec((1, d), lambda i: (0, 0))],
        out_specs=pl.BlockSpec((tm, d), lambda i: (i, 0)),
        out_shape=jax.ShapeDtypeStruct((m, d), F32),
        compiler_params=_cparams(("parallel",)), name="final_rmsnorm",
    )(x, g.reshape(1, d))


def _log_sigmoid(x):
    return jnp.minimum(x, 0.0) - jnp.log1p(jnp.exp(-jnp.abs(x)))


def _mlstm_kernel(q_ref, k_ref, v_ref, gz_ref, bias_ref, c0_ref, n0_ref, m0_ref,
                  h_ref, c_out, n_out, m_out, c_scr, n_scr, m_scr):
    c = pl.program_id(1)
    L = q_ref.shape[1]

    @pl.when(c == 0)
    def _():
        c_scr[...] = c0_ref[0]
        n_scr[...] = n0_ref[0]
        m_scr[...] = m0_ref[0]

    row = lax.broadcasted_iota(I32, (L, L), 0)
    col = lax.broadcasted_iota(I32, (L, L), 1)
    tril = col <= row
    eye = col == row
    pre = gz_ref[0] + bias_ref[...]
    neg_inf = jnp.float32(-jnp.inf)

    for h in range(HEADS):
        sl = slice(h * HEAD_DIM, (h + 1) * HEAD_DIM)
        q = q_ref[0, :, sl]
        k = k_ref[0, :, sl] * (HEAD_DIM ** -0.5)
        v = v_ref[0, :, sl]
        qb, kb, vb = q.astype(BF16), k.astype(BF16), v.astype(BF16)
        ig_c = pre[:, h:h + 1]
        lf_c = _log_sigmoid(pre[:, HEADS + h:HEADS + h + 1])
        ig_r = jnp.sum(jnp.where(eye, ig_c, 0.0), axis=0, keepdims=True)
        lf_r = jnp.sum(jnp.where(eye, lf_c, 0.0), axis=0, keepdims=True)
        bt_c = jnp.sum(jnp.where(tril, lf_r, 0.0), axis=1, keepdims=True)
        bt_r = jnp.sum(jnp.where(row <= col, lf_c, 0.0), axis=0, keepdims=True)
        m_prev = m_scr[h:h + 1, 0:1]
        logw = jnp.where(tril, bt_c - bt_r + ig_r, neg_inf)
        inter = bt_c + m_prev
        m_t = jnp.maximum(inter, jnp.max(logw, axis=1, keepdims=True))
        w = jnp.exp(logw - m_t)
        a_inter = jnp.exp(inter - m_t)
        s = lax.dot_general(qb, kb, (((1,), (1,)), ((), ())), preferred_element_type=F32)
        sw = s * w
        cm = c_scr[h]
        nm = n_scr[h:h + 1, :]
        num = a_inter * jnp.dot(qb, cm.astype(BF16), preferred_element_type=F32) \
            + jnp.dot(sw.astype(BF16), vb, preferred_element_type=F32)
        den = a_inter * jnp.sum(q * nm, axis=1, keepdims=True) + jnp.sum(sw, axis=1, keepdims=True)
        h_ref[0, :, sl] = num / jnp.maximum(jnp.abs(den), jnp.exp(-m_t))
        m_new = m_t[L - 1:L, :]
        bt_last = bt_c[L - 1:L, :]
        g_state = jnp.exp(bt_last + m_prev - m_new)
        w_s = jnp.exp(bt_last - bt_c + ig_c - m_new)
        kw = k * w_s
        c_scr[h] = g_state * cm + lax.dot_general(kw.astype(BF16), vb, (((0,), (0,)), ((), ())),
                                                   preferred_element_type=F32)
        n_scr[h:h + 1, :] = g_state * nm + jnp.sum(kw, axis=0, keepdims=True)
        m_scr[h:h + 1, :] = jnp.broadcast_to(m_new, (1, LANES))

    @pl.when(c == pl.num_programs(1) - 1)
    def _():
        c_out[0] = c_scr[...]
        n_out[0] = n_scr[...]
        m_out[0] = m_scr[...]


def mlstm_scan(z, gz, bias, c0, n0, m0):
    b, t, _ = z.shape
    nc = t // CHUNK
    blk = lambda j: pl.BlockSpec((1, CHUNK, D_MODEL), lambda i, c: (i, c, j))
    st4 = pl.BlockSpec((1, HEADS, HEAD_DIM, HEAD_DIM), lambda i, c: (i, 0, 0, 0))
    st3 = pl.BlockSpec((1, HEADS, HEAD_DIM), lambda i, c: (i, 0, 0))
    stm = pl.BlockSpec((1, HEADS, LANES), lambda i, c: (i, 0, 0))
    m0b = jnp.broadcast_to(m0[:, :, None], (b, HEADS, LANES))
    h, c_new, n_new, m_new = pl.pallas_call(
        _mlstm_kernel, grid=(b, nc),
        in_specs=[blk(0), blk(1), blk(2),
                  pl.BlockSpec((1, CHUNK, LANES), lambda i, c: (i, c, 0)),
                  pl.BlockSpec((1, LANES), lambda i, c: (0, 0)), st4, st3, stm],
        out_specs=[pl.BlockSpec((1, CHUNK, D_MODEL), lambda i, c: (i, c, 0)), st4, st3, stm],
        out_shape=[jax.ShapeDtypeStruct((b, t, D_MODEL), F32),
                   jax.ShapeDtypeStruct((b, HEADS, HEAD_DIM, HEAD_DIM), F32),
                   jax.ShapeDtypeStruct((b, HEADS, HEAD_DIM), F32),
                   jax.ShapeDtypeStruct((b, HEADS, LANES), F32)],
        scratch_shapes=[pltpu.VMEM((HEADS, HEAD_DIM, HEAD_DIM), F32),
                        pltpu.VMEM((HEADS, HEAD_DIM), F32),
                        pltpu.VMEM((HEADS, LANES), F32)],
        compiler_params=_cparams(("parallel", "arbitrary")), name="mlstm_scan",
    )(z, z, z, gz, bias, c0, n0, m0b)
    return h, c_new, n_new, m_new[:, :, 0]


def _mlstm_out_kernel(h_ref, o_ref, x_ref, g_ref, w_ref, out_ref):
    hh = h_ref[...]
    parts = []
    for i in range(HEADS):
        a = hh[:, i * HEAD_DIM:(i + 1) * HEAD_DIM]
        d = a - jnp.mean(a, axis=-1, keepdims=True)
        parts.append(d * lax.rsqrt(jnp.mean(d * d, axis=-1, keepdims=True) + EPS))
    hn = jnp.concatenate(parts, axis=-1) * g_ref[...]
    y = (hn * jax.nn.sigmoid(o_ref[...])).astype(BF16)
    out_ref[...] = x_ref[...] + jnp.dot(y, w_ref[...], preferred_element_type=F32)


def mlstm_out(h, z, x, g_head, w_out, tm=512):
    m, d = x.shape
    tm = min(tm, m)
    row = pl.BlockSpec((tm, d), lambda i: (i, 0))
    return pl.pallas_call(
        _mlstm_out_kernel, grid=(m // tm,),
        in_specs=[row, pl.BlockSpec((tm, d), lambda i: (i, 3)), row,
                  pl.BlockSpec((1, d), lambda i: (0, 0)), pl.BlockSpec((d, d), lambda i: (0, 0))],
        out_specs=row, out_shape=jax.ShapeDtypeStruct((m, d), F32),
        compiler_params=_cparams(("parallel",)), name="mlstm_out",
    )(h, z, x, g_head.reshape(1, d), w_out)


def _pool_kernel(x_ref, pre_ref, g_ref, w_ref, s_ref, out_ref, tail_ref, ext_scr, *, pos0):
    t = pl.program_id(1)
    tb = x_ref.shape[1]
    x = x_ref[0]
    u = _rms(x, g_ref[...])

    @pl.when(t == 0)
    def _():
        ext_scr[0:POOL_HALO, :] = pre_ref[0]

    ext_scr[POOL_HALO:POOL_HALO + tb, :] = u
    pos = pos0 + t * tb + lax.broadcasted_iota(I32, (tb, 1), 0)
    ys = []
    for gi, win in enumerate(POOL_WINDOWS):
        sl = slice(gi * POOL_GROUP, (gi + 1) * POOL_GROUP)
        s = ext_scr[:, sl]
        sh = 1
        while sh < win:
            s = s + pltpu.roll(s, sh, 0)
            sh *= 2
        cnt = jnp.minimum(pos + 1, win).astype(F32)
        p = s[POOL_HALO:, :] / cnt - u[:, sl]
        ys.append(jnp.dot(p.astype(BF16), w_ref[gi], preferred_element_type=F32))
    out_ref[0] = x + jnp.concatenate(ys, axis=-1) * s_ref[...]
    tail = u[tb - POOL_HALO:, :]
    tail_ref[0] = tail
    ext_scr[0:POOL_HALO, :] = tail


def pool_mix(x, prefix, pos0, g, w_pool, s_pool, tb=256):
    b, t, d = x.shape
    tb = min(tb, t)
    return pl.pallas_call(
        functools.partial(_pool_kernel, pos0=pos0), grid=(b, t // tb),
        in_specs=[pl.BlockSpec((1, tb, d), lambda i, j: (i, j, 0)),
                  pl.BlockSpec((1, POOL_HALO, d), lambda i, j: (i, 0, 0)),
                  pl.BlockSpec((1, d), lambda i, j: (0, 0)),
                  pl.BlockSpec(w_pool.shape, lambda i, j: (0, 0, 0)),
                  pl.BlockSpec((1, d), lambda i, j: (0, 0))],
        out_specs=[pl.BlockSpec((1, tb, d), lambda i, j: (i, j, 0)),
                   pl.BlockSpec((1, POOL_HALO, d), lambda i, j: (i, 0, 0))],
        out_shape=[jax.ShapeDtypeStruct((b, t, d), F32), jax.ShapeDtypeStruct((b, POOL_HALO, d), F32)],
        scratch_shapes=[pltpu.VMEM((POOL_HALO + tb, d), F32)],
        compiler_params=_cparams(("parallel", "arbitrary")), name="pool_mix",
    )(x, prefix, g.reshape(1, d), w_pool, s_pool.reshape(1, d))


def _xattn_kernel(x_ref, k_ref, v_ref, g_ref, wq_ref, wo_ref, out_ref):
    x = x_ref[0]
    u = _rms(x, g_ref[...]).astype(BF16)
    q = jnp.dot(u, wq_ref[...], preferred_element_type=F32).astype(BF16)
    outs = []
    for h in range(HEADS):
        sl = slice(h * HEAD_DIM, (h + 1) * HEAD_DIM)
        s = lax.dot_general(q[:, sl], k_ref[0, :, sl], (((1,), (1,)), ((), ())),
                            preferred_element_type=F32) * (HEAD_DIM ** -0.5)
        e = jnp.exp(s - jnp.max(s, axis=-1, keepdims=True))
        p = e / jnp.sum(e, axis=-1, keepdims=True)
        outs.append(jnp.dot(p.astype(BF16), v_ref[0, :, sl], preferred_element_type=F32))
    o = jnp.concatenate(outs, axis=-1).astype(BF16)
    out_ref[0] = x + jnp.dot(o, wo_ref[...], preferred_element_type=F32)


def xattn(x, k, v, g, w_q, w_o, tq=512):
    b, t, d = x.shape
    tq = min(tq, t)
    xs = pl.BlockSpec((1, tq, d), lambda i, j: (i, j, 0))
    kv = pl.BlockSpec((1, N_MEM, d), lambda i, j: (i, 0, 0))
    wsp = pl.BlockSpec((d, d), lambda i, j: (0, 0))
    return pl.pallas_call(
        _xattn_kernel, grid=(b, t // tq),
        in_specs=[xs, kv, kv, pl.BlockSpec((1, d), lambda i, j: (0, 0)), wsp, wsp],
        out_specs=xs, out_shape=jax.ShapeDtypeStruct((b, t, d), F32),
        compiler_params=_cparams(("parallel", "arbitrary")), name="xattn",
    )(x, k, v, g.reshape(1, d), w_q, w_o)


def _top16_rows(problems, emits):
    neg_inf = jnp.float32(-jnp.inf)
    tb = problems[0][0][0].shape[1]
    row8 = lax.broadcasted_iota(I32, (SUBLANES, tb), 0)
    tags_of = [tags for _, tags in problems]

    def body(r, carry):
        out = []
        for vals, tags, emit in zip(carry, tags_of, emits):
            rows = [row8 + SUBLANES * i for i in range(len(vals))]
            cur = [(v, rw, None if tags is None else tg) for v, rw, tg in zip(vals, rows, tags or rows)]
            while len(cur) > 1:
                nxt = []
                for j in range(0, len(cur) - 1, 2):
                    (va, ra, ta), (vb, rb, tb_) = cur[j], cur[j + 1]
                    c = va >= vb
                    nxt.append((jnp.where(c, va, vb), jnp.where(c, ra, rb),
                                None if ta is None else jnp.where(c, ta, tb_)))
                if len(cur) % 2:
                    nxt.append(cur[-1])
                cur = nxt
            v, rw, tg = cur[0]
            for sh in (4, 2, 1):
                v2, r2 = pltpu.roll(v, sh, 0), pltpu.roll(rw, sh, 0)
                c = (v > v2) | ((v == v2) & (rw < r2))
                if tg is not None:
                    tg = jnp.where(c, tg, pltpu.roll(tg, sh, 0))
                v, rw = jnp.where(c, v, v2), jnp.where(c, rw, r2)
            emit(r, v[0:1, :], (rw if tg is None else tg)[0:1, :])
            out.append(tuple(jnp.where(rows[i] == rw, neg_inf, vals[i]) for i in range(len(vals))))
        return tuple(out)

    lax.fori_loop(0, PEER_TOPK, body, tuple(tuple(vals) for vals, _ in problems))


_N_CAND_ROWS = PEER_TOPK + 7 * SUBLANES + SUBLANES


def _route_kernel(q_ref, keys_ref, eidx_ref, gate_ref, s_scr, val_scr, idx_scr, top_scr, sel_scr):
    tb = q_ref.shape[0]
    row8 = lax.broadcasted_iota(I32, (SUBLANES, tb), 0)
    neg_inf = jnp.float32(-jnp.inf)

    for hp in range(2 * PEER_HEADS):
        s_scr[hp] = lax.dot_general(keys_ref[hp], q_ref[:, hp * PEER_HALF:(hp + 1) * PEER_HALF],
                                    (((1,), (1,)), ((), ())), preferred_element_type=F32)

    def scores(hp):
        return [s_scr[hp, i:i + SUBLANES, :] for i in range(0, PEER_N_KEYS, SUBLANES)], None

    def emit1(j):
        def emit(r, mx, pos):
            val_scr[j, pl.ds(r, 1), :] = mx
            idx_scr[j, pl.ds(r, 1), :] = pos
        return emit

    def candidates(j):
        sv0, sv1, si0, si1 = val_scr[j], val_scr[j + 1], idx_scr[j], idx_scr[j + 1]
        lo8, hi8 = slice(0, SUBLANES), slice(SUBLANES, PEER_TOPK)
        vals = [sv0[0:1, :] + sv1[lo8, :], sv0[0:1, :] + sv1[hi8, :]]
        ids = [si0[0:1, :] * PEER_N_KEYS + si1[lo8, :], si0[0:1, :] * PEER_N_KEYS + si1[hi8, :]]
        for a in range(1, SUBLANES):
            keep = row8 < PEER_TOPK // (a + 1)
            vals.append(jnp.where(keep, sv0[a:a + 1, :] + sv1[lo8, :], neg_inf))
            ids.append(si0[a:a + 1, :] * PEER_N_KEYS + si1[lo8, :])
        vals.append(sv0[hi8, :] + sv1[0:1, :])
        ids.append(si0[hi8, :] * PEER_N_KEYS + si1[0:1, :])
        return vals, ids

    def emit2(j):
        def emit(r, mx, expert):
            top_scr[j, pl.ds(r, 1), :] = mx
            sel_scr[j, pl.ds(r, 1), :] = expert
        return emit

    for h0 in range(0, PEER_HEADS, 2):
        for j in range(2):
            hp = 2 * (h0 + j)
            _top16_rows([scores(hp), scores(hp + 1)], [emit1(2 * j), emit1(2 * j + 1)])
        _top16_rows([candidates(0), candidates(2)], [emit2(0), emit2(1)])
        for j in range(2):
            top = top_scr[j]
            e = jnp.exp(top - jnp.max(top, axis=0, keepdims=True))
            rows = slice((h0 + j) * PEER_TOPK, (h0 + j + 1) * PEER_TOPK)
            gate_ref[rows, :] = e / jnp.sum(e, axis=0, keepdims=True)
            eidx_ref[rows, :] = sel_scr[j]


def peer_route(q, keys, tb=128):
    m = q.shape[0]
    out = pl.BlockSpec((PEER_SLOTS, tb), lambda i: (0, i))
    return pl.pallas_call(
        _route_kernel, grid=(m // tb,),
        in_specs=[pl.BlockSpec((tb, q.shape[1]), lambda i: (i, 0)),
                  pl.BlockSpec(keys.shape, lambda i: (0, 0, 0))],
        out_specs=[out, out],
        out_shape=[jax.ShapeDtypeStruct((PEER_SLOTS, m), I32), jax.ShapeDtypeStruct((PEER_SLOTS, m), F32)],
        scratch_shapes=[pltpu.VMEM((2 * PEER_HEADS, PEER_N_KEYS, tb), F32),
                        pltpu.VMEM((4, PEER_TOPK, tb), F32), pltpu.VMEM((4, PEER_TOPK, tb), I32),
                        pltpu.VMEM((2, PEER_TOPK, tb), F32), pltpu.VMEM((2, PEER_TOPK, tb), I32)],
        compiler_params=_cparams(("parallel",)), name="peer_route",
    )(q, keys)


GROUP = 8
SLOT_UNROLL = LANES // GROUP
SLOT_TRIPS = PEER_SLOTS // SLOT_UNROLL
_HI_MASK = 0xFFFF0000


def _unpack(w):
    lo = pltpu.bitcast(w << 16, F32)
    hi = pltpu.bitcast(w & jnp.uint32(_HI_MASK), F32)
    return lo, hi


def _expert_in_kernel(idx_ref, xg_ref, gate_ref, tab_ref, w_ref, part_scr):
    tbu = gate_ref.shape[0]
    sub = lax.broadcasted_iota(I32, (SUBLANES, LANES), 0)
    lane = lax.broadcasted_iota(I32, (SUBLANES, LANES), 1)
    m2 = (sub % 4) < 2
    m1 = (sub % 2) < 1
    pair_tokens = ((0, 4), (2, 6), (1, 5), (3, 7))

    def group(g, carry):
        base = pl.multiple_of(g * (GROUP * SUBLANES), GROUP * SUBLANES)
        xs = [xg_ref[pl.ds(base + j * SUBLANES, SUBLANES), :] for j in range(GROUP)]

        def slots(kb, c):
            for kk in range(SLOT_UNROLL):
                k = kb * SLOT_UNROLL + kk
                f = []
                for j, (ta, tb_) in enumerate(pair_tokens):
                    ia = pl.multiple_of(idx_ref[g * SLOT_TRIPS + kb, kk * GROUP + ta], ROW_SUBLANES)
                    ib = pl.multiple_of(idx_ref[g * SLOT_TRIPS + kb, kk * GROUP + tb_], ROW_SUBLANES)
                    w = jnp.concatenate([tab_ref[pl.ds(ia, ROW_SUBLANES), :],
                                         tab_ref[pl.ds(ib, ROW_SUBLANES), :]], axis=0)
                    lo, hi = _unpack(w)
                    f.append(lo * xs[2 * j] + hi * xs[2 * j + 1])
                gs = []
                for a, b in ((f[0], f[1]), (f[2], f[3])):
                    gs.append(jnp.where(m2, a, pltpu.roll(b, 2, 0)) + jnp.where(m2, pltpu.roll(a, 6, 0), b))
                part_scr[pl.ds(pl.multiple_of(k * SUBLANES, SUBLANES), SUBLANES), :] = (
                    jnp.where(m1, gs[0], pltpu.roll(gs[1], 1, 0)) + jnp.where(m1, pltpu.roll(gs[0], 7, 0), gs[1]))
            return c

        lax.fori_loop(0, SLOT_TRIPS, slots, 0)
        accs = [jnp.zeros((SUBLANES, LANES), F32) for _ in range(4)]
        for k in range(PEER_SLOTS):
            tot = jnp.sum(part_scr[k * SUBLANES:(k + 1) * SUBLANES, :], axis=1, keepdims=True)
            accs[k % 4] = jnp.where(lane == k, tot, accs[k % 4])
        act = (accs[0] + accs[1]) + (accs[2] + accs[3])
        rows = pl.ds(pl.multiple_of(g * GROUP, GROUP), GROUP)
        gelu = 0.5 * act * (1.0 + lax.erf(act * (1.0 / math.sqrt(2.0))))
        w_ref[rows, :] = gate_ref[rows, :] * gelu
        return carry

    lax.fori_loop(0, tbu // GROUP, group, 0)


def expert_in(idx_g, xg, gate, tab, tbu=64):
    m = gate.shape[0]
    tbu = min(tbu, m)
    return pl.pallas_call(
        _expert_in_kernel, grid=(m // tbu,),
        in_specs=[pl.BlockSpec((tbu, LANES), lambda i: (i, 0), memory_space=pltpu.SMEM),
                  pl.BlockSpec((tbu * SUBLANES, LANES), lambda i: (i, 0)),
                  pl.BlockSpec((tbu, PEER_SLOTS), lambda i: (i, 0)),
                  pl.BlockSpec(tab.shape, lambda i: (0, 0), pipeline_mode=pl.Buffered(1))],
        out_specs=pl.BlockSpec((tbu, PEER_SLOTS), lambda i: (i, 0)),
        out_shape=jax.ShapeDtypeStruct((m, PEER_SLOTS), F32),
        scratch_shapes=[pltpu.VMEM((PEER_SLOTS * SUBLANES, LANES), F32)],
        compiler_params=_cparams(("arbitrary",), VMEM_LIMIT), name="expert_in",
    )(idx_g, xg, gate, tab)


def _expert_out_kernel(idx_ref, w_ref, x8_ref, tab_ref, out_ref, wrep_a, wrep_b):
    tbv = idx_ref.shape[0]
    n_acc = 4

    def spread(wrep, t):
        wrep[...] = jnp.broadcast_to(w_ref[pl.ds(t, 1), :], (PEER_SLOTS, LANES)).T

    def token(t, wrep):
        lo = [jnp.zeros((ROW_SUBLANES, LANES), F32) for _ in range(n_acc)]
        hi = [jnp.zeros((ROW_SUBLANES, LANES), F32) for _ in range(n_acc)]
        for k in range(PEER_SLOTS):
            i = pl.multiple_of(idx_ref[t, k], ROW_SUBLANES)
            a, b = _unpack(tab_ref[pl.ds(i, ROW_SUBLANES), :])
            wk = wrep[k:k + 1, :]
            lo[k % n_acc] = lo[k % n_acc] + wk * a
            hi[k % n_acc] = hi[k % n_acc] + wk * b
        lo_s = (lo[0] + lo[1]) + (lo[2] + lo[3])
        hi_s = (hi[0] + hi[1]) + (hi[2] + hi[3])
        rows = pl.ds(pl.multiple_of(t * SUBLANES, SUBLANES), SUBLANES)
        out_ref[rows, :] = x8_ref[rows, :] + jnp.concatenate([lo_s, hi_s], axis=0)

    def pair(i, carry):
        spread(wrep_b, 2 * i + 1)
        token(2 * i, wrep_a)
        spread(wrep_a, jnp.minimum(2 * i + 2, tbv - 1))
        token(2 * i + 1, wrep_b)
        return carry

    spread(wrep_a, 0)
    lax.fori_loop(0, tbv // 2, pair, 0)


def expert_out(idx4, w, x8, tab, tbv=64):
    m = w.shape[0]
    tbv = min(tbv, m)
    rows = pl.BlockSpec((tbv * SUBLANES, LANES), lambda i: (i, 0))
    return pl.pallas_call(
        _expert_out_kernel, grid=(m // tbv,),
        in_specs=[pl.BlockSpec((tbv, PEER_SLOTS), lambda i: (i, 0), memory_space=pltpu.SMEM),
                  pl.BlockSpec((tbv, PEER_SLOTS), lambda i: (i, 0)), rows,
                  pl.BlockSpec(tab.shape, lambda i: (0, 0), pipeline_mode=pl.Buffered(1))],
        out_specs=rows, out_shape=jax.ShapeDtypeStruct(x8.shape, F32),
        scratch_shapes=[pltpu.VMEM((PEER_SLOTS, LANES), F32), pltpu.VMEM((PEER_SLOTS, LANES), F32)],
        compiler_params=_cparams(("arbitrary",), VMEM_LIMIT), name="expert_out",
    )(idx4, w, x8, tab)


def _pack_table(t):
    bits = lax.bitcast_convert_type(t.astype(BF16), jnp.uint16).astype(U32)
    packed = bits[:, :ROW_WORDS] | (bits[:, ROW_WORDS:] << 16)
    return packed.reshape(t.shape[0] * ROW_SUBLANES, LANES)


def _group_tokens(u):
    m = u.shape[0]
    a = u.reshape(m // GROUP, 2, 2, 2, 2, ROW_SUBLANES, LANES)
    a = a.transpose(0, 3, 2, 4, 1, 5, 6)
    return a.reshape(m * SUBLANES, LANES)


def peer(x, g, w_q, keys, tab_u, tab_v):
    m = x.shape[0]
    q, u = norm_matmul(x, g, [w_q], [BF16], emit_u=True)
    eidx_t, gate_t = peer_route(q, keys, tb=min(128, m))
    idx4 = eidx_t.T * ROW_SUBLANES
    idx_g = idx4.reshape(m // GROUP, GROUP, PEER_SLOTS).transpose(0, 2, 1).reshape(m, LANES)
    w = expert_in(idx_g, _group_tokens(u), gate_t.T, tab_u)
    out8 = expert_out(idx4, w, x.reshape(m * SUBLANES, LANES), tab_v)
    return out8.reshape(m, D_MODEL)


def kernel(x_prompt, x_sample, state_mlstm_C, state_mlstm_n, state_mlstm_m, state_pool, cache_mem_k, cache_mem_v, mem_prompt, g_mix, w_mlstm_in, b_mlstm_i, b_mlstm_f, g_mlstm_head, w_mlstm_out, w_pool, s_pool, g_xattn, g_mem, w_xq, w_xk, w_xv, w_xo, g_ffn, w_peer_q, peer_keys, peer_u, peer_v, g_final):
    d = D_MODEL
    bp, tp, _ = x_prompt.shape
    bs, ts, _ = x_sample.shape
    depth = g_mix.shape[0]
    groups = [dict(x=x_prompt.reshape(bp * tp, d), b=bp, t=tp),
              dict(x=x_sample.reshape(bs * ts, d), b=bs, t=ts)]
    outs = [dict(C=[], n=[], m=[], pool=[]) for _ in groups]
    mk_p, mv_p = [], []
    mem = mem_prompt.reshape(bp * N_MEM, d)

    for i in range(depth):
        j = i // 2
        if i % 2 == 0:
            w_in = w_mlstm_in[j]
            w_main = w_in[:, :4 * d].astype(BF16)
            w_gate = jnp.pad(w_in[:, 4 * d:], ((0, 0), (0, LANES - 2 * HEADS)))
            bias = jnp.pad(jnp.concatenate([b_mlstm_i[j], b_mlstm_f[j]]), (0, LANES - 2 * HEADS)).reshape(1, LANES)
            w_out = w_mlstm_out[j].astype(BF16)
        else:
            w_pl = w_pool[j].astype(BF16)
        w_kv = [w_xk[i].astype(BF16), w_xv[i].astype(BF16)]
        w_q, w_o = w_xq[i].astype(BF16), w_xo[i].astype(BF16)
        w_pq = w_peer_q[i].astype(BF16)
        keys = peer_keys[i].reshape(2 * PEER_HEADS, PEER_N_KEYS, PEER_HALF).astype(BF16)
        tab_u, tab_v = _pack_table(peer_u[i]), _pack_table(peer_v[i])

        kp, vp = norm_matmul(mem, g_mem[i], w_kv, [F32, F32])
        mk_p.append(kp.reshape(bp, N_MEM, HEADS, HEAD_DIM))
        mv_p.append(vp.reshape(bp, N_MEM, HEADS, HEAD_DIM))
        kvs = [(kp.reshape(bp, N_MEM, d).astype(BF16), vp.reshape(bp, N_MEM, d).astype(BF16)),
               (cache_mem_k[i].reshape(bs, N_MEM, d).astype(BF16), cache_mem_v[i].reshape(bs, N_MEM, d).astype(BF16))]

        for gi, (grp, out) in enumerate(zip(groups, outs)):
            x, b, t = grp["x"], grp["b"], grp["t"]
            if i % 2 == 0:
                z, gz = norm_matmul(x, g_mix[i], [w_main, w_gate], [F32, F32], hi=(False, True))
                if gi == 0:
                    c0 = jnp.zeros((b, HEADS, HEAD_DIM, HEAD_DIM), F32)
                    n0 = jnp.zeros((b, HEADS, HEAD_DIM), F32)
                    m0 = jnp.zeros((b, HEADS), F32)
                else:
                    c0, n0, m0 = state_mlstm_C[j], state_mlstm_n[j], state_mlstm_m[j]
                h, c1, n1, m1 = mlstm_scan(z.reshape(b, t, 4 * d), gz.reshape(b, t, LANES), bias, c0, n0, m0)
                out["C"].append(c1); out["n"].append(n1); out["m"].append(m1)
                x = mlstm_out(h.reshape(b * t, d), z, x, g_mlstm_head[j], w_out)
            else:
                if gi == 0:
                    prefix, pos0 = jnp.zeros((b, POOL_HALO, d), F32), 0
                else:
                    prefix, pos0 = jnp.pad(state_pool[j], ((0, 0), (1, 0), (0, 0))), 4096
                x3, tail = pool_mix(x.reshape(b, t, d), prefix, pos0, g_mix[i], w_pl, s_pool[j])
                out["pool"].append(tail[:, 1:])
                x = x3.reshape(b * t, d)
            kb, vb = kvs[gi]
            x = xattn(x.reshape(b, t, d), kb, vb, g_xattn[i], w_q, w_o).reshape(b * t, d)
            x = peer(x, g_ffn[i], w_pq, keys, tab_u, tab_v)
            grp["x"] = x

    y = [rmsnorm(grp["x"], g_final).reshape(grp["b"], grp["t"], d) for grp in groups]
    st = lambda lst: jnp.stack(lst)
    po, so = outs
    return (y[0], y[1],
            st(po["C"]), st(po["n"]), st(po["m"]), st(po["pool"]), st(mk_p), st(mv_p),
            st(so["C"]), st(so["n"]), st(so["m"]), st(so["pool"]))
```

```python
import functools
import math

import jax
import jax.numpy as jnp
from jax import lax
from jax.experimental import pallas as pl
from jax.experimental.pallas import tpu as pltpu

F32 = jnp.float32
BF16 = jnp.bfloat16
I32 = jnp.int32
U32 = jnp.uint32
EPS = 1e-6
HIGHEST = lax.Precision.HIGHEST

D_MODEL = 1024
CHUNK = 64
HEADS = 4
HEAD_DIM = D_MODEL // HEADS
POOL_WINDOWS = (2, 4, 8, 16)
POOL_GROUP = D_MODEL // len(POOL_WINDOWS)
POOL_HALO = 16
N_MEM = 256
PEER_HEADS = 8
PEER_N_KEYS = 128
PEER_HALF = 128
PEER_TOPK = 16
PEER_SLOTS = PEER_HEADS * PEER_TOPK
LANES = 128
SUBLANES = 8
ROW_WORDS = D_MODEL // 2
ROW_SUBLANES = ROW_WORDS // LANES
VMEM_LIMIT = 56 * 1024 * 1024


def _cparams(sem, vmem=None):
    return pltpu.CompilerParams(dimension_semantics=sem, vmem_limit_bytes=vmem)


def _rms(x, g):
    return x * lax.rsqrt(jnp.mean(x * x, axis=-1, keepdims=True) + EPS) * g


def _norm_matmul_kernel(*refs, n_w, hi, emit_u, n_chunk):
    x_ref, g_ref = refs[0], refs[1]
    w_refs = refs[2:2 + n_w]
    o_refs = refs[2 + n_w:]
    u = _rms(x_ref[...], g_ref[...])
    ub = u.astype(BF16)
    for w_ref, o_ref, h in zip(w_refs, o_refs, hi):
        n = w_ref.shape[1]
        step = min(n, n_chunk)
        for j in range(0, n, step):
            if h:
                o = jnp.dot(u, w_ref[:, j:j + step], precision=HIGHEST, preferred_element_type=F32)
            else:
                o = jnp.dot(ub, w_ref[:, j:j + step], preferred_element_type=F32)
            o_ref[:, j:j + step] = o.astype(o_ref.dtype)
    if emit_u:
        o_refs[n_w][...] = u


def norm_matmul(x, g, ws, out_dtypes, hi=None, emit_u=False, tm=512):
    m, d = x.shape
    tm = min(tm, m)
    hi = tuple(hi or (False,) * len(ws))
    in_specs = [pl.BlockSpec((tm, d), lambda i: (i, 0)), pl.BlockSpec((1, d), lambda i: (0, 0))]
    in_specs += [pl.BlockSpec(w.shape, lambda i: (0, 0)) for w in ws]
    out_shape = [jax.ShapeDtypeStruct((m, w.shape[1]), dt) for w, dt in zip(ws, out_dtypes)]
    out_specs = [pl.BlockSpec((tm, w.shape[1]), lambda i: (i, 0)) for w in ws]
    if emit_u:
        out_shape.append(jax.ShapeDtypeStruct((m, d), F32))
        out_specs.append(pl.BlockSpec((tm, d), lambda i: (i, 0)))
    return pl.pallas_call(
        functools.partial(_norm_matmul_kernel, n_w=len(ws), hi=hi, emit_u=emit_u, n_chunk=512),
        grid=(m // tm,), in_specs=in_specs, out_specs=out_specs, out_shape=out_shape,
        compiler_params=_cparams(("parallel",), VMEM_LIMIT), name="norm_matmul",
    )(x, g.reshape(1, d), *ws)


def _rmsnorm_kernel(x_ref, g_ref, o_ref):
    o_ref[...] = _rms(x_ref[...], g_ref[...])


def rmsnorm(x, g, tm=512):
    m, d = x.shape
    tm = min(tm, m)
    return pl.pallas_call(
        _rmsnorm_kernel, grid=(m // tm,),
        in_specs=[pl.BlockSpec((tm, d), lambda i: (i, 0)), pl.BlockSpec((1, d), lambda i: (0, 0))],
        out_specs=pl.BlockSpec((tm, d), lambda i: (i, 0)),
        out_shape=jax.ShapeDtypeStruct((m, d), F32),
        compiler_params=_cparams(("parallel",)), name="final_rmsnorm",
    )(x, g.reshape(1, d))


def _log_sigmoid(x):
    return jnp.minimum(x, 0.0) - jnp.log1p(jnp.exp(-jnp.abs(x)))


def _mlstm_kernel(q_ref, k_ref, v_ref, gz_ref, bias_ref, c0_ref, n0_ref, m0_ref,
                  h_ref, c_out, n_out, m_out, c_scr, n_scr, m_scr):
    c = pl.program_id(1)
    L = q_ref.shape[1]

    @pl.when(c == 0)
    def _():
        c_scr[...] = c0_ref[0]
        n_scr[...] = n0_ref[0]
        m_scr[...] = m0_ref[0]

    row = lax.broadcasted_iota(I32, (L, L), 0)
    col = lax.broadcasted_iota(I32, (L, L), 1)
    tril = col <= row
    eye = col == row
    pre = gz_ref[0] + bias_ref[...]
    neg_inf = jnp.float32(-jnp.inf)

    for h in range(HEADS):
        sl = slice(h * HEAD_DIM, (h + 1) * HEAD_DIM)
        q = q_ref[0, :, sl]
        k = k_ref[0, :, sl] * (HEAD_DIM ** -0.5)
        v = v_ref[0, :, sl]
        qb, kb, vb = q.astype(BF16), k.astype(BF16), v.astype(BF16)
        ig_c = pre[:, h:h + 1]
        lf_c = _log_sigmoid(pre[:, HEADS + h:HEADS + h + 1])
        ig_r = jnp.sum(jnp.where(eye, ig_c, 0.0), axis=0, keepdims=True)
        lf_r = jnp.sum(jnp.where(eye, lf_c, 0.0), axis=0, keepdims=True)
        bt_c = jnp.sum(jnp.where(tril, lf_r, 0.0), axis=1, keepdims=True)
        bt_r = jnp.sum(jnp.where(row <= col, lf_c, 0.0), axis=0, keepdims=True)
        m_prev = m_scr[h:h + 1, 0:1]
        logw = jnp.where(tril, bt_c - bt_r + ig_r, neg_inf)
        inter = bt_c + m_prev
        m_t = jnp.maximum(inter, jnp.max(logw, axis=1, keepdims=True))
        w = jnp.exp(logw - m_t)
        a_inter = jnp.exp(inter - m_t)
        s = lax.dot_general(qb, kb, (((1,), (1,)), ((), ())), preferred_element_type=F32)
        sw = s * w
        cm = c_scr[h]
        nm = n_scr[h:h + 1, :]
        num = a_inter * jnp.dot(qb, cm.astype(BF16), preferred_element_type=F32) \
            + jnp.dot(sw.astype(BF16), vb, preferred_element_type=F32)
        den = a_inter * jnp.sum(q * nm, axis=1, keepdims=True) + jnp.sum(sw, axis=1, keepdims=True)
        h_ref[0, :, sl] = num / jnp.maximum(jnp.abs(den), jnp.exp(-m_t))
        m_new = m_t[L - 1:L, :]
        bt_last = bt_c[L - 1:L, :]
        g_state = jnp.exp(bt_last + m_prev - m_new)
        w_s = jnp.exp(bt_last - bt_c + ig_c - m_new)
        kw = k * w_s
        c_scr[h] = g_state * cm + lax.dot_general(kw.astype(BF16), vb, (((0,), (0,)), ((), ())),
                                                   preferred_element_type=F32)
        n_scr[h:h + 1, :] = g_state * nm + jnp.sum(kw, axis=0, keepdims=True)
        m_scr[h:h + 1, :] = jnp.broadcast_to(m_new, (1, LANES))

    @pl.when(c == pl.num_programs(1) - 1)
    def _():
        c_out[0] = c_scr[...]
        n_out[0] = n_scr[...]
        m_out[0] = m_scr[...]


def mlstm_scan(z, gz, bias, c0, n0, m0):
    b, t, _ = z.shape
    nc = t // CHUNK
    blk = lambda j: pl.BlockSpec((1, CHUNK, D_MODEL), lambda i, c: (i, c, j))
    st4 = pl.BlockSpec((1, HEADS, HEAD_DIM, HEAD_DIM), lambda i, c: (i, 0, 0, 0))
    st3 = pl.BlockSpec((1, HEADS, HEAD_DIM), lambda i, c: (i, 0, 0))
    stm = pl.BlockSpec((1, HEADS, LANES), lambda i, c: (i, 0, 0))
    m0b = jnp.broadcast_to(m0[:, :, None], (b, HEADS, LANES))
    h, c_new, n_new, m_new = pl.pallas_call(
        _mlstm_kernel, grid=(b, nc),
        in_specs=[blk(0), blk(1), blk(2),
                  pl.BlockSpec((1, CHUNK, LANES), lambda i, c: (i, c, 0)),
                  pl.BlockSpec((1, LANES), lambda i, c: (0, 0)), st4, st3, stm],
        out_specs=[pl.BlockSpec((1, CHUNK, D_MODEL), lambda i, c: (i, c, 0)), st4, st3, stm],
        out_shape=[jax.ShapeDtypeStruct((b, t, D_MODEL), F32),
                   jax.ShapeDtypeStruct((b, HEADS, HEAD_DIM, HEAD_DIM), F32),
                   jax.ShapeDtypeStruct((b, HEADS, HEAD_DIM), F32),
                   jax.ShapeDtypeStruct((b, HEADS, LANES), F32)],
        scratch_shapes=[pltpu.VMEM((HEADS, HEAD_DIM, HEAD_DIM), F32),
                        pltpu.VMEM((HEADS, HEAD_DIM), F32),
                        pltpu.VMEM((HEADS, LANES), F32)],
        compiler_params=_cparams(("parallel", "arbitrary")), name="mlstm_scan",
    )(z, z, z, gz, bias, c0, n0, m0b)
    return h, c_new, n_new, m_new[:, :, 0]


def _mlstm_out_kernel(h_ref, o_ref, x_ref, g_ref, w_ref, out_ref):
    hh = h_ref[...]
    parts = []
    for i in range(HEADS):
        a = hh[:, i * HEAD_DIM:(i + 1) * HEAD_DIM]
        d = a - jnp.mean(a, axis=-1, keepdims=True)
        parts.append(d * lax.rsqrt(jnp.mean(d * d, axis=-1, keepdims=True) + EPS))
    hn = jnp.concatenate(parts, axis=-1) * g_ref[...]
    y = (hn * jax.nn.sigmoid(o_ref[...])).astype(BF16)
    out_ref[...] = x_ref[...] + jnp.dot(y, w_ref[...], preferred_element_type=F32)


def mlstm_out(h, z, x, g_head, w_out, tm=512):
    m, d = x.shape
    tm = min(tm, m)
    row = pl.BlockSpec((tm, d), lambda i: (i, 0))
    return pl.pallas_call(
        _mlstm_out_kernel, grid=(m // tm,),
        in_specs=[row, pl.BlockSpec((tm, d), lambda i: (i, 3)), row,
                  pl.BlockSpec((1, d), lambda i: (0, 0)), pl.BlockSpec((d, d), lambda i: (0, 0))],
        out_specs=row, out_shape=jax.ShapeDtypeStruct((m, d), F32),
        compiler_params=_cparams(("parallel",)), name="mlstm_out",
    )(h, z, x, g_head.reshape(1, d), w_out)


def _pool_kernel(x_ref, pre_ref, g_ref, w_ref, s_ref, out_ref, tail_ref, ext_scr, *, pos0):
    t = pl.program_id(1)
    tb = x_ref.shape[1]
    x = x_ref[0]
    u = _rms(x, g_ref[...])

    @pl.when(t == 0)
    def _():
        ext_scr[0:POOL_HALO, :] = pre_ref[0]

    ext_scr[POOL_HALO:POOL_HALO + tb, :] = u
    pos = pos0 + t * tb + lax.broadcasted_iota(I32, (tb, 1), 0)
    ys = []
    for gi, win in enumerate(POOL_WINDOWS):
        sl = slice(gi * POOL_GROUP, (gi + 1) * POOL_GROUP)
        s = ext_scr[:, sl]
        sh = 1
        while sh < win:
            s = s + pltpu.roll(s, sh, 0)
            sh *= 2
        cnt = jnp.minimum(pos + 1, win).astype(F32)
        p = s[POOL_HALO:, :] / cnt - u[:, sl]
        ys.append(jnp.dot(p.astype(BF16), w_ref[gi], preferred_element_type=F32))
    out_ref[0] = x + jnp.concatenate(ys, axis=-1) * s_ref[...]
    tail = u[tb - POOL_HALO:, :]
    tail_ref[0] = tail
    ext_scr[0:POOL_HALO, :] = tail


def pool_mix(x, prefix, pos0, g, w_pool, s_pool, tb=256):
    b, t, d = x.shape
    tb = min(tb, t)
    return pl.pallas_call(
        functools.partial(_pool_kernel, pos0=pos0), grid=(b, t // tb),
        in_specs=[pl.BlockSpec((1, tb, d), lambda i, j: (i, j, 0)),
                  pl.BlockSpec((1, POOL_HALO, d), lambda i, j: (i, 0, 0)),
                  pl.BlockSpec((1, d), lambda i, j: (0, 0)),
                  pl.BlockSpec(w_pool.shape, lambda i, j: (0, 0, 0)),
                  pl.BlockSpec((1, d), lambda i, j: (0, 0))],
        out_specs=[pl.BlockSpec((1, tb, d), lambda i, j: (i, j, 0)),
                   pl.BlockSpec((1, POOL_HALO, d), lambda i, j: (i, 0, 0))],
        out_shape=[jax.ShapeDtypeStruct((b, t, d), F32), jax.ShapeDtypeStruct((b, POOL_HALO, d), F32)],
        scratch_shapes=[pltpu.VMEM((POOL_HALO + tb, d), F32)],
        compiler_params=_cparams(("parallel", "arbitrary")), name="pool_mix",
    )(x, prefix, g.reshape(1, d), w_pool, s_pool.reshape(1, d))


def _xattn_kernel(x_ref, k_ref, v_ref, g_ref, wq_ref, wo_ref, out_ref):
    x = x_ref[0]
    u = _rms(x, g_ref[...]).astype(BF16)
    q = jnp.dot(u, wq_ref[...], preferred_element_type=F32).astype(BF16)
    outs = []
    for h in range(HEADS):
        sl = slice(h * HEAD_DIM, (h + 1) * HEAD_DIM)
        s = lax.dot_general(q[:, sl], k_ref[0, :, sl], (((1,), (1,)), ((), ())),
                            preferred_element_type=F32) * (HEAD_DIM ** -0.5)
        e = jnp.exp(s - jnp.max(s, axis=-1, keepdims=True))
        p = e / jnp.sum(e, axis=-1, keepdims=True)
        outs.append(jnp.dot(p.astype(BF16), v_ref[0, :, sl], preferred_element_type=F32))
    o = jnp.concatenate(outs, axis=-1).astype(BF16)
    out_ref[0] = x + jnp.dot(o, wo_ref[...], preferred_element_type=F32)


def xattn(x, k, v, g, w_q, w_o, tq=512):
    b, t, d = x.shape
    tq = min(tq, t)
    xs = pl.BlockSpec((1, tq, d), lambda i, j: (i, j, 0))
    kv = pl.BlockSpec((1, N_MEM, d), lambda i, j: (i, 0, 0))
    wsp = pl.BlockSpec((d, d), lambda i, j: (0, 0))
    return pl.pallas_call(
        _xattn_kernel, grid=(b, t // tq),
        in_specs=[xs, kv, kv, pl.BlockSpec((1, d), lambda i, j: (0, 0)), wsp, wsp],
        out_specs=xs, out_shape=jax.ShapeDtypeStruct((b, t, d), F32),
        compiler_params=_cparams(("parallel", "arbitrary")), name="xattn",
    )(x, k, v, g.reshape(1, d), w_q, w_o)


def _top16_rows(problems, emits):
    neg_inf = jnp.float32(-jnp.inf)
    tb = problems[0][0][0].shape[1]
    row8 = lax.broadcasted_iota(I32, (SUBLANES, tb), 0)
    tags_of = [tags for _, tags in problems]

    def body(r, carry):
        out = []
        for vals, tags, emit in zip(carry, tags_of, emits):
            rows = [row8 + SUBLANES * i for i in range(len(vals))]
            cur = [(v, rw, None if tags is None else tg) for v, rw, tg in zip(vals, rows, tags or rows)]
            while len(cur) > 1:
                nxt = []
                for j in range(0, len(cur) - 1, 2):
                    (va, ra, ta), (vb, rb, tb_) = cur[j], cur[j + 1]
                    c = va >= vb
                    nxt.append((jnp.where(c, va, vb), jnp.where(c, ra, rb),
                                None if ta is None else jnp.where(c, ta, tb_)))
                if len(cur) % 2:
                    nxt.append(cur[-1])
                cur = nxt
            v, rw, tg = cur[0]
            for sh in (4, 2, 1):
                v2, r2 = pltpu.roll(v, sh, 0), pltpu.roll(rw, sh, 0)
                c = (v > v2) | ((v == v2) & (rw < r2))
                if tg is not None:
                    tg = jnp.where(c, tg, pltpu.roll(tg, sh, 0))
                v, rw = jnp.where(c, v, v2), jnp.where(c, rw, r2)
            emit(r, v[0:1, :], (rw if tg is None else tg)[0:1, :])
            out.append(tuple(jnp.where(rows[i] == rw, neg_inf, vals[i]) for i in range(len(vals))))
        return tuple(out)

    lax.fori_loop(0, PEER_TOPK, body, tuple(tuple(vals) for vals, _ in problems))


_N_CAND_ROWS = PEER_TOPK + 7 * SUBLANES + SUBLANES


def _route_kernel(q_ref, keys_ref, eidx_ref, gate_ref, s_scr, val_scr, idx_scr, top_scr, sel_scr):
    tb = q_ref.shape[0]
    row8 = lax.broadcasted_iota(I32, (SUBLANES, tb), 0)
    neg_inf = jnp.float32(-jnp.inf)

    for hp in range(2 * PEER_HEADS):
        s_scr[hp] = lax.dot_general(keys_ref[hp], q_ref[:, hp * PEER_HALF:(hp + 1) * PEER_HALF],
                                    (((1,), (1,)), ((), ())), preferred_element_type=F32)

    def scores(hp):
        return [s_scr[hp, i:i + SUBLANES, :] for i in range(0, PEER_N_KEYS, SUBLANES)], None

    def emit1(j):
        def emit(r, mx, pos):
            val_scr[j, pl.ds(r, 1), :] = mx
            idx_scr[j, pl.ds(r, 1), :] = pos
        return emit

    def candidates(j):
        sv0, sv1, si0, si1 = val_scr[j], val_scr[j + 1], idx_scr[j], idx_scr[j + 1]
        lo8, hi8 = slice(0, SUBLANES), slice(SUBLANES, PEER_TOPK)
        vals = [sv0[0:1, :] + sv1[lo8, :], sv0[0:1, :] + sv1[hi8, :]]
        ids = [si0[0:1, :] * PEER_N_KEYS + si1[lo8, :], si0[0:1, :] * PEER_N_KEYS + si1[hi8, :]]
        for a in range(1, SUBLANES):
            keep = row8 < PEER_TOPK // (a + 1)
            vals.append(jnp.where(keep, sv0[a:a + 1, :] + sv1[lo8, :], neg_inf))
            ids.append(si0[a:a + 1, :] * PEER_N_KEYS + si1[lo8, :])
        vals.append(sv0[hi8, :] + sv1[0:1, :])
        ids.append(si0[hi8, :] * PEER_N_KEYS + si1[0:1, :])
        return vals, ids

    def emit2(j):
        def emit(r, mx, expert):
            top_scr[j, pl.ds(r, 1), :] = mx
            sel_scr[j, pl.ds(r, 1), :] = expert
        return emit

    for h0 in range(0, PEER_HEADS, 2):
        for j in range(2):
            hp = 2 * (h0 + j)
            _top16_rows([scores(hp), scores(hp + 1)], [emit1(2 * j), emit1(2 * j + 1)])
        _top16_rows([candidates(0), candidates(2)], [emit2(0), emit2(1)])
        for j in range(2):
            top = top_scr[j]
            e = jnp.exp(top - jnp.max(top, axis=0, keepdims=True))
            rows = slice((h0 + j) * PEER_TOPK, (h0 + j + 1) * PEER_TOPK)
            gate_ref[rows, :] = e / jnp.sum(e, axis=0, keepdims=True)
            eidx_ref[rows, :] = sel_scr[j]


def peer_route(q, keys, tb=128):
    m = q.shape[0]
    out = pl.BlockSpec((PEER_SLOTS, tb), lambda i: (0, i))
    return pl.pallas_call(
        _route_kernel, grid=(m // tb,),
        in_specs=[pl.BlockSpec((tb, q.shape[1]), lambda i: (i, 0)),
                  pl.BlockSpec(keys.shape, lambda i: (0, 0, 0))],
        out_specs=[out, out],
        out_shape=[jax.ShapeDtypeStruct((PEER_SLOTS, m), I32), jax.ShapeDtypeStruct((PEER_SLOTS, m), F32)],
        scratch_shapes=[pltpu.VMEM((2 * PEER_HEADS, PEER_N_KEYS, tb), F32),
                        pltpu.VMEM((4, PEER_TOPK, tb), F32), pltpu.VMEM((4, PEER_TOPK, tb), I32),
                        pltpu.VMEM((2, PEER_TOPK, tb), F32), pltpu.VMEM((2, PEER_TOPK, tb), I32)],
        compiler_params=_cparams(("parallel",)), name="peer_route",
    )(q, keys)


GROUP = 8
SLOT_UNROLL = LANES // GROUP
SLOT_TRIPS = PEER_SLOTS // SLOT_UNROLL
_HI_MASK = 0xFFFF0000


def _unpack(w):
    lo = pltpu.bitcast(w << 16, F32)
    hi = pltpu.bitcast(w & jnp.uint32(_HI_MASK), F32)
    return lo, hi


def _transpose8(tiles, sub):
    t = list(tiles)
    for d in (4, 2, 1):
        keep = (sub & d) == 0
        for i in range(SUBLANES):
            if i & d == 0:
                a, b = t[i], t[i + d]
                t[i] = jnp.where(keep, a, pltpu.roll(b, d, 0))
                t[i + d] = jnp.where(keep, pltpu.roll(a, SUBLANES - d, 0), b)
    return t


def _split_halves(a, b, sub):
    keep = sub < ROW_SUBLANES
    return (jnp.where(keep, a, pltpu.roll(b, ROW_SUBLANES, 0)),
            jnp.where(keep, pltpu.roll(a, ROW_SUBLANES, 0), b))


def _expert_in_kernel(idx_ref, x_ref, gate_ref, tab_ref, w_ref, part_a, part_b):
    tbu = gate_ref.shape[0]
    sub = lax.broadcasted_iota(I32, (SUBLANES, LANES), 0)
    lane = lax.broadcasted_iota(I32, (SUBLANES, LANES), 1)
    m2 = (sub % 4) < 2
    m1 = (sub % 2) < 1
    pair_tokens = ((0, 4), (2, 6), (1, 5), (3, 7))

    def finish(g, accs):
        act = (accs[0] + accs[1]) + (accs[2] + accs[3])
        rows = pl.ds(pl.multiple_of(g * GROUP, GROUP), GROUP)
        gelu = 0.5 * act * (1.0 + lax.erf(act * (1.0 / math.sqrt(2.0))))
        w_ref[rows, :] = gate_ref[rows, :] * gelu

    def lane_sums(part, k, accs, j):
        tile = part[pl.ds(pl.multiple_of(k * SUBLANES, SUBLANES), SUBLANES), :]
        accs[j] = jnp.where(lane == k, jnp.sum(tile, axis=1, keepdims=True), accs[j])

    def group(g, part, prev):
        toks = pl.ds(pl.multiple_of(g * GROUP, GROUP), GROUP)
        per_token = _transpose8([x_ref[toks, c * LANES:(c + 1) * LANES] for c in range(SUBLANES)], sub)
        xs = []
        for ta, tb_ in pair_tokens:
            xs.extend(_split_halves(per_token[ta], per_token[tb_], sub))

        rows_of = [idx_ref.at[g * GROUP + t] for t in range(GROUP)]

        def slots(kb, accs):
            accs = list(accs)
            for kk in range(SLOT_UNROLL):
                k = kb * SLOT_UNROLL + kk
                lane_sums(prev, k, accs, kk % 4)
                f = []
                for j, (ta, tb_) in enumerate(pair_tokens):
                    ia = pl.multiple_of(rows_of[ta][k], ROW_SUBLANES)
                    ib = pl.multiple_of(rows_of[tb_][k], ROW_SUBLANES)
                    w = jnp.concatenate([tab_ref[pl.ds(ia, ROW_SUBLANES), :],
                                         tab_ref[pl.ds(ib, ROW_SUBLANES), :]], axis=0)
                    lo, hi = _unpack(w)
                    f.append(lo * xs[2 * j] + hi * xs[2 * j + 1])
                gs = []
                for a, b in ((f[0], f[1]), (f[2], f[3])):
                    gs.append(jnp.where(m2, a, pltpu.roll(b, 2, 0)) + jnp.where(m2, pltpu.roll(a, 6, 0), b))
                part[pl.ds(pl.multiple_of(k * SUBLANES, SUBLANES), SUBLANES), :] = (
                    jnp.where(m1, gs[0], pltpu.roll(gs[1], 1, 0)) + jnp.where(m1, pltpu.roll(gs[0], 7, 0), gs[1]))
            return tuple(accs)

        return lax.fori_loop(0, SLOT_TRIPS, slots, tuple(jnp.zeros((SUBLANES, LANES), F32) for _ in range(4)))

    part_b[...] = jnp.zeros(part_b.shape, F32)

    def group_pair(i, carry):
        accs = group(2 * i, part_a, part_b)

        @pl.when(i > 0)
        def _():
            finish(2 * i - 1, accs)

        finish(2 * i, group(2 * i + 1, part_b, part_a))
        return carry

    lax.fori_loop(0, tbu // (2 * GROUP), group_pair, 0)
    accs = [jnp.zeros((SUBLANES, LANES), F32) for _ in range(4)]
    for k in range(PEER_SLOTS):
        lane_sums(part_b, k, accs, k % 4)
    finish(tbu // GROUP - 1, accs)


def expert_in(idx4, u, gate, tab, tbu=128):
    m = gate.shape[0]
    tbu = min(tbu, m)
    return pl.pallas_call(
        _expert_in_kernel, grid=(m // tbu,),
        in_specs=[pl.BlockSpec((tbu, LANES), lambda i: (i, 0), memory_space=pltpu.SMEM),
                  pl.BlockSpec((tbu, D_MODEL), lambda i: (i, 0)),
                  pl.BlockSpec((tbu, PEER_SLOTS), lambda i: (i, 0)),
                  pl.BlockSpec(tab.shape, lambda i: (0, 0), pipeline_mode=pl.Buffered(1))],
        out_specs=pl.BlockSpec((tbu, PEER_SLOTS), lambda i: (i, 0)),
        out_shape=jax.ShapeDtypeStruct((m, PEER_SLOTS), F32),
        scratch_shapes=[pltpu.VMEM((PEER_SLOTS * SUBLANES, LANES), F32)] * 2,
        compiler_params=_cparams(("arbitrary",), VMEM_LIMIT), name="expert_in",
    )(idx4, u, gate, tab)


def _expert_out_kernel(idx_ref, w_ref, x_ref, tab_ref, out_ref, wrep_a, wrep_b):
    tbv = idx_ref.shape[0]
    sub = lax.broadcasted_iota(I32, (SUBLANES, LANES), 0)
    lo_half = sub < ROW_SUBLANES
    pair_tokens = ((0, 1), (2, 3), (4, 5), (6, 7))

    def spread(wrep, tok, slot):
        col = jnp.broadcast_to(w_ref[pl.ds(tok, 1), :], (PEER_SLOTS, LANES)).T
        wrep[pl.ds(pl.multiple_of(slot * PEER_SLOTS, PEER_SLOTS), PEER_SLOTS), :] = col

    def group(g, wrep, wrep_next):
        rows_of = [idx_ref.at[g * GROUP + t] for t in range(GROUP)]

        def slots(kb, accs):
            accs = list(accs)
            spread(wrep_next, jnp.minimum((g + 1) * GROUP + kb, tbv - 1), kb)
            wbase = pl.multiple_of(kb * SLOT_UNROLL, SLOT_UNROLL)
            for kk in range(SLOT_UNROLL):
                k = kb * SLOT_UNROLL + kk
                for j, (ta, tb_) in enumerate(pair_tokens):
                    ia = pl.multiple_of(rows_of[ta][k], ROW_SUBLANES)
                    ib = pl.multiple_of(rows_of[tb_][k], ROW_SUBLANES)
                    lo, hi = _unpack(jnp.concatenate([tab_ref[pl.ds(ia, ROW_SUBLANES), :],
                                                      tab_ref[pl.ds(ib, ROW_SUBLANES), :]], axis=0))
                    wt = jnp.where(lo_half, wrep[pl.ds(wbase + (ta * PEER_SLOTS + kk), 1), :],
                                   wrep[pl.ds(wbase + (tb_ * PEER_SLOTS + kk), 1), :])
                    accs[2 * j] = accs[2 * j] + wt * lo
                    accs[2 * j + 1] = accs[2 * j + 1] + wt * hi
            return tuple(accs)

        accs = lax.fori_loop(0, SLOT_TRIPS, slots,
                             tuple(jnp.zeros((SUBLANES, LANES), F32) for _ in range(GROUP)))
        per_token = [None] * GROUP
        for j, (ta, tb_) in enumerate(pair_tokens):
            per_token[ta], per_token[tb_] = _split_halves(accs[2 * j], accs[2 * j + 1], sub)
        toks = pl.ds(pl.multiple_of(g * GROUP, GROUP), GROUP)
        for c, tile in enumerate(_transpose8(per_token, sub)):
            cols = slice(c * LANES, (c + 1) * LANES)
            out_ref[toks, cols] = x_ref[toks, cols] + tile

    def group_pair(i, carry):
        group(2 * i, wrep_a, wrep_b)
        group(2 * i + 1, wrep_b, wrep_a)
        return carry

    for t in range(GROUP):
        spread(wrep_a, t, t)
    lax.fori_loop(0, tbv // (2 * GROUP), group_pair, 0)


def expert_out(idx4, w, x, tab, tbv=128):
    m = w.shape[0]
    tbv = min(tbv, m)
    rows = pl.BlockSpec((tbv, D_MODEL), lambda i: (i, 0))
    return pl.pallas_call(
        _expert_out_kernel, grid=(m // tbv,),
        in_specs=[pl.BlockSpec((tbv, PEER_SLOTS), lambda i: (i, 0), memory_space=pltpu.SMEM),
                  pl.BlockSpec((tbv, PEER_SLOTS), lambda i: (i, 0)), rows,
                  pl.BlockSpec(tab.shape, lambda i: (0, 0), pipeline_mode=pl.Buffered(1))],
        out_specs=rows, out_shape=jax.ShapeDtypeStruct(x.shape, F32),
        scratch_shapes=[pltpu.VMEM((GROUP * PEER_SLOTS, LANES), F32)] * 2,
        compiler_params=_cparams(("arbitrary",), VMEM_LIMIT), name="expert_out",
    )(idx4, w, x, tab)


def _pack_table(t):
    bits = lax.bitcast_convert_type(t.astype(BF16), jnp.uint16).astype(U32)
    packed = bits[:, :ROW_WORDS] | (bits[:, ROW_WORDS:] << 16)
    return packed.reshape(t.shape[0] * ROW_SUBLANES, LANES)


def peer(x, g, w_q, keys, tab_u, tab_v):
    m = x.shape[0]
    q, u = norm_matmul(x, g, [w_q], [BF16], emit_u=True)
    eidx_t, gate_t = peer_route(q, keys, tb=min(128, m))
    idx4 = eidx_t.T * ROW_SUBLANES
    w = expert_in(idx4, u, gate_t.T, tab_u)
    return expert_out(idx4, w, x, tab_v)


def kernel(x_prompt, x_sample, state_mlstm_C, state_mlstm_n, state_mlstm_m, state_pool, cache_mem_k, cache_mem_v, mem_prompt, g_mix, w_mlstm_in, b_mlstm_i, b_mlstm_f, g_mlstm_head, w_mlstm_out, w_pool, s_pool, g_xattn, g_mem, w_xq, w_xk, w_xv, w_xo, g_ffn, w_peer_q, peer_keys, peer_u, peer_v, g_final):
    d = D_MODEL
    bp, tp, _ = x_prompt.shape
    bs, ts, _ = x_sample.shape
    depth = g_mix.shape[0]
    groups = [dict(x=x_prompt.reshape(bp * tp, d), b=bp, t=tp),
              dict(x=x_sample.reshape(bs * ts, d), b=bs, t=ts)]
    outs = [dict(C=[], n=[], m=[], pool=[]) for _ in groups]
    mk_p, mv_p = [], []
    mem = mem_prompt.reshape(bp * N_MEM, d)

    for i in range(depth):
        j = i // 2
        if i % 2 == 0:
            w_in = w_mlstm_in[j]
            w_main = w_in[:, :4 * d].astype(BF16)
            w_gate = jnp.pad(w_in[:, 4 * d:], ((0, 0), (0, LANES - 2 * HEADS)))
            bias = jnp.pad(jnp.concatenate([b_mlstm_i[j], b_mlstm_f[j]]), (0, LANES - 2 * HEADS)).reshape(1, LANES)
            w_out = w_mlstm_out[j].astype(BF16)
        else:
            w_pl = w_pool[j].astype(BF16)
        w_kv = [w_xk[i].astype(BF16), w_xv[i].astype(BF16)]
        w_q, w_o = w_xq[i].astype(BF16), w_xo[i].astype(BF16)
        w_pq = w_peer_q[i].astype(BF16)
        keys = peer_keys[i].reshape(2 * PEER_HEADS, PEER_N_KEYS, PEER_HALF).astype(BF16)
        tab_u, tab_v = _pack_table(peer_u[i]), _pack_table(peer_v[i])

        kp, vp = norm_matmul(mem, g_mem[i], w_kv, [F32, F32])
        mk_p.append(kp.reshape(bp, N_MEM, HEADS, HEAD_DIM))
        mv_p.append(vp.reshape(bp, N_MEM, HEADS, HEAD_DIM))
        kvs = [(kp.reshape(bp, N_MEM, d).astype(BF16), vp.reshape(bp, N_MEM, d).astype(BF16)),
               (cache_mem_k[i].reshape(bs, N_MEM, d).astype(BF16), cache_mem_v[i].reshape(bs, N_MEM, d).astype(BF16))]

        for gi, (grp, out) in enumerate(zip(groups, outs)):
            x, b, t = grp["x"], grp["b"], grp["t"]
            if i % 2 == 0:
                z, gz = norm_matmul(x, g_mix[i], [w_main, w_gate], [F32, F32], hi=(False, True))
                if gi == 0:
                    c0 = jnp.zeros((b, HEADS, HEAD_DIM, HEAD_DIM), F32)
                    n0 = jnp.zeros((b, HEADS, HEAD_DIM), F32)
                    m0 = jnp.zeros((b, HEADS), F32)
                else:
                    c0, n0, m0 = state_mlstm_C[j], state_mlstm_n[j], state_mlstm_m[j]
                h, c1, n1, m1 = mlstm_scan(z.reshape(b, t, 4 * d), gz.reshape(b, t, LANES), bias, c0, n0, m0)
                out["C"].append(c1); out["n"].append(n1); out["m"].append(m1)
                x = mlstm_out(h.reshape(b * t, d), z, x, g_mlstm_head[j], w_out)
            else:
                if gi == 0:
                    prefix, pos0 = jnp.zeros((b, POOL_HALO, d), F32), 0
                else:
                    prefix, pos0 = jnp.pad(state_pool[j], ((0, 0), (1, 0), (0, 0))), 4096
                x3, tail = pool_mix(x.reshape(b, t, d), prefix, pos0, g_mix[i], w_pl, s_pool[j])
                out["pool"].append(tail[:, 1:])
                x = x3.reshape(b * t, d)
            kb, vb = kvs[gi]
            x = xattn(x.reshape(b, t, d), kb, vb, g_xattn[i], w_q, w_o).reshape(b * t, d)
            x = peer(x, g_ffn[i], w_pq, keys, tab_u, tab_v)
            grp["x"] = x

    y = [rmsnorm(grp["x"], g_final).reshape(grp["b"], grp["t"], d) for grp in groups]
    st = lambda lst: jnp.stack(lst)
    po, so = outs
    return (y[0], y[1],
            st(po["C"]), st(po["n"]), st(po["m"]), st(po["pool"]), st(mk_p), st(mv_p),
            st(so["C"]), st(so["n"]), st(so["m"]), st(so["pool"]))
```

```python
import functools
import math

import jax
import jax.numpy as jnp
from jax import lax
from jax.experimental import pallas as pl
from jax.experimental.pallas import tpu as pltpu

F32 = jnp.float32
BF16 = jnp.bfloat16
I32 = jnp.int32
U32 = jnp.uint32
EPS = 1e-6
HIGHEST = lax.Precision.HIGHEST

D_MODEL = 1024
CHUNK = 64
HEADS = 4
HEAD_DIM = D_MODEL // HEADS
POOL_WINDOWS = (2, 4, 8, 16)
POOL_GROUP = D_MODEL // len(POOL_WINDOWS)
POOL_HALO = 16
N_MEM = 256
PEER_HEADS = 8
PEER_N_KEYS = 128
PEER_HALF = 128
PEER_TOPK = 16
PEER_SLOTS = PEER_HEADS * PEER_TOPK
LANES = 128
SUBLANES = 8
ROW_WORDS = D_MODEL // 2
ROW_SUBLANES = ROW_WORDS // LANES
TABLE_PAD = SUBLANES - ROW_SUBLANES
VMEM_LIMIT = 56 * 1024 * 1024


def _cparams(sem, vmem=None):
    return pltpu.CompilerParams(dimension_semantics=sem, vmem_limit_bytes=vmem)


def _rms(x, g):
    return x * lax.rsqrt(jnp.mean(x * x, axis=-1, keepdims=True) + EPS) * g


def _norm_matmul_kernel(*refs, n_w, hi, emit_u, n_chunk):
    x_ref, g_ref = refs[0], refs[1]
    w_refs = refs[2:2 + n_w]
    o_refs = refs[2 + n_w:]
    u = _rms(x_ref[...], g_ref[...])
    ub = u.astype(BF16)
    for w_ref, o_ref, h in zip(w_refs, o_refs, hi):
        n = w_ref.shape[1]
        step = min(n, n_chunk)
        for j in range(0, n, step):
            if h:
                o = jnp.dot(u, w_ref[:, j:j + step], precision=HIGHEST, preferred_element_type=F32)
            else:
                o = jnp.dot(ub, w_ref[:, j:j + step], preferred_element_type=F32)
            o_ref[:, j:j + step] = o.astype(o_ref.dtype)
    if emit_u:
        o_refs[n_w][...] = u


def norm_matmul(x, g, ws, out_dtypes, hi=None, emit_u=False, tm=512):
    m, d = x.shape
    tm = min(tm, m)
    hi = tuple(hi or (False,) * len(ws))
    in_specs = [pl.BlockSpec((tm, d), lambda i: (i, 0)), pl.BlockSpec((1, d), lambda i: (0, 0))]
    in_specs += [pl.BlockSpec(w.shape, lambda i: (0, 0)) for w in ws]
    out_shape = [jax.ShapeDtypeStruct((m, w.shape[1]), dt) for w, dt in zip(ws, out_dtypes)]
    out_specs = [pl.BlockSpec((tm, w.shape[1]), lambda i: (i, 0)) for w in ws]
    if emit_u:
        out_shape.append(jax.ShapeDtypeStruct((m, d), F32))
        out_specs.append(pl.BlockSpec((tm, d), lambda i: (i, 0)))
    return pl.pallas_call(
        functools.partial(_norm_matmul_kernel, n_w=len(ws), hi=hi, emit_u=emit_u, n_chunk=512),
        grid=(m // tm,), in_specs=in_specs, out_specs=out_specs, out_shape=out_shape,
        compiler_params=_cparams(("parallel",), VMEM_LIMIT), name="norm_matmul",
    )(x, g.reshape(1, d), *ws)


def _rmsnorm_kernel(x_ref, g_ref, o_ref):
    o_ref[...] = _rms(x_ref[...], g_ref[...])


def rmsnorm(x, g, tm=512):
    m, d = x.shape
    tm = min(tm, m)
    return pl.pallas_call(
        _rmsnorm_kernel, grid=(m // tm,),
        in_specs=[pl.BlockSpec((tm, d), lambda i: (i, 0)), pl.BlockSpec((1, d), lambda i: (0, 0))],
        out_specs=pl.BlockSpec((tm, d), lambda i: (i, 0)),
        out_shape=jax.ShapeDtypeStruct((m, d), F32),
        compiler_params=_cparams(("parallel",)), name="final_rmsnorm",
    )(x, g.reshape(1, d))


def _log_sigmoid(x):
    return jnp.minimum(x, 0.0) - jnp.log1p(jnp.exp(-jnp.abs(x)))


def _mlstm_kernel(q_ref, k_ref, v_ref, gz_ref, bias_ref, c0_ref, n0_ref, m0_ref,
                  h_ref, c_out, n_out, m_out, c_scr, n_scr, m_scr):
    c = pl.program_id(1)
    L = q_ref.shape[1]

    @pl.when(c == 0)
    def _():
        c_scr[...] = c0_ref[0]
        n_scr[...] = n0_ref[0]
        m_scr[...] = m0_ref[0]

    row = lax.broadcasted_iota(I32, (L, L), 0)
    col = lax.broadcasted_iota(I32, (L, L), 1)
    tril = col <= row
    eye = col == row
    pre = gz_ref[0] + bias_ref[...]
    neg_inf = jnp.float32(-jnp.inf)

    for h in range(HEADS):
        sl = slice(h * HEAD_DIM, (h + 1) * HEAD_DIM)
        q = q_ref[0, :, sl]
        k = k_ref[0, :, sl] * (HEAD_DIM ** -0.5)
        v = v_ref[0, :, sl]
        qb, kb, vb = q.astype(BF16), k.astype(BF16), v.astype(BF16)
        ig_c = pre[:, h:h + 1]
        lf_c = _log_sigmoid(pre[:, HEADS + h:HEADS + h + 1])
        ig_r = jnp.sum(jnp.where(eye, ig_c, 0.0), axis=0, keepdims=True)
        lf_r = jnp.sum(jnp.where(eye, lf_c, 0.0), axis=0, keepdims=True)
        bt_c = jnp.sum(jnp.where(tril, lf_r, 0.0), axis=1, keepdims=True)
        bt_r = jnp.sum(jnp.where(row <= col, lf_c, 0.0), axis=0, keepdims=True)
        m_prev = m_scr[h:h + 1, 0:1]
        logw = jnp.where(tril, bt_c - bt_r + ig_r, neg_inf)
        inter = bt_c + m_prev
        m_t = jnp.maximum(inter, jnp.max(logw, axis=1, keepdims=True))
        w = jnp.exp(logw - m_t)
        a_inter = jnp.exp(inter - m_t)
        s = lax.dot_general(qb, kb, (((1,), (1,)), ((), ())), preferred_element_type=F32)
        sw = s * w
        cm = c_scr[h]
        nm = n_scr[h:h + 1, :]
        num = a_inter * jnp.dot(qb, cm.astype(BF16), preferred_element_type=F32) \
            + jnp.dot(sw.astype(BF16), vb, preferred_element_type=F32)
        den = a_inter * jnp.sum(q * nm, axis=1, keepdims=True) + jnp.sum(sw, axis=1, keepdims=True)
        h_ref[0, :, sl] = num / jnp.maximum(jnp.abs(den), jnp.exp(-m_t))
        m_new = m_t[L - 1:L, :]
        bt_last = bt_c[L - 1:L, :]
        g_state = jnp.exp(bt_last + m_prev - m_new)
        w_s = jnp.exp(bt_last - bt_c + ig_c - m_new)
        kw = k * w_s
        c_scr[h] = g_state * cm + lax.dot_general(kw.astype(BF16), vb, (((0,), (0,)), ((), ())),
                                                   preferred_element_type=F32)
        n_scr[h:h + 1, :] = g_state * nm + jnp.sum(kw, axis=0, keepdims=True)
        m_scr[h:h + 1, :] = jnp.broadcast_to(m_new, (1, LANES))

    @pl.when(c == pl.num_programs(1) - 1)
    def _():
        c_out[0] = c_scr[...]
        n_out[0] = n_scr[...]
        m_out[0] = m_scr[...]


def mlstm_scan(z, gz, bias, c0, n0, m0):
    b, t, _ = z.shape
    nc = t // CHUNK
    blk = lambda j: pl.BlockSpec((1, CHUNK, D_MODEL), lambda i, c: (i, c, j))
    st4 = pl.BlockSpec((1, HEADS, HEAD_DIM, HEAD_DIM), lambda i, c: (i, 0, 0, 0))
    st3 = pl.BlockSpec((1, HEADS, HEAD_DIM), lambda i, c: (i, 0, 0))
    stm = pl.BlockSpec((1, HEADS, LANES), lambda i, c: (i, 0, 0))
    m0b = jnp.broadcast_to(m0[:, :, None], (b, HEADS, LANES))
    h, c_new, n_new, m_new = pl.pallas_call(
        _mlstm_kernel, grid=(b, nc),
        in_specs=[blk(0), blk(1), blk(2),
                  pl.BlockSpec((1, CHUNK, LANES), lambda i, c: (i, c, 0)),
                  pl.BlockSpec((1, LANES), lambda i, c: (0, 0)), st4, st3, stm],
        out_specs=[pl.BlockSpec((1, CHUNK, D_MODEL), lambda i, c: (i, c, 0)), st4, st3, stm],
        out_shape=[jax.ShapeDtypeStruct((b, t, D_MODEL), F32),
                   jax.ShapeDtypeStruct((b, HEADS, HEAD_DIM, HEAD_DIM), F32),
                   jax.ShapeDtypeStruct((b, HEADS, HEAD_DIM), F32),
                   jax.ShapeDtypeStruct((b, HEADS, LANES), F32)],
        scratch_shapes=[pltpu.VMEM((HEADS, HEAD_DIM, HEAD_DIM), F32),
                        pltpu.VMEM((HEADS, HEAD_DIM), F32),
                        pltpu.VMEM((HEADS, LANES), F32)],
        compiler_params=_cparams(("parallel", "arbitrary")), name="mlstm_scan",
    )(z, z, z, gz, bias, c0, n0, m0b)
    return h, c_new, n_new, m_new[:, :, 0]


def _mlstm_out_kernel(h_ref, o_ref, x_ref, g_ref, w_ref, out_ref):
    hh = h_ref[...]
    parts = []
    for i in range(HEADS):
        a = hh[:, i * HEAD_DIM:(i + 1) * HEAD_DIM]
        d = a - jnp.mean(a, axis=-1, keepdims=True)
        parts.append(d * lax.rsqrt(jnp.mean(d * d, axis=-1, keepdims=True) + EPS))
    hn = jnp.concatenate(parts, axis=-1) * g_ref[...]
    y = (hn * jax.nn.sigmoid(o_ref[...])).astype(BF16)
    out_ref[...] = x_ref[...] + jnp.dot(y, w_ref[...], preferred_element_type=F32)


def mlstm_out(h, z, x, g_head, w_out, tm=512):
    m, d = x.shape
    tm = min(tm, m)
    row = pl.BlockSpec((tm, d), lambda i: (i, 0))
    return pl.pallas_call(
        _mlstm_out_kernel, grid=(m // tm,),
        in_specs=[row, pl.BlockSpec((tm, d), lambda i: (i, 3)), row,
                  pl.BlockSpec((1, d), lambda i: (0, 0)), pl.BlockSpec((d, d), lambda i: (0, 0))],
        out_specs=row, out_shape=jax.ShapeDtypeStruct((m, d), F32),
        compiler_params=_cparams(("parallel",)), name="mlstm_out",
    )(h, z, x, g_head.reshape(1, d), w_out)


def _pool_kernel(x_ref, pre_ref, g_ref, w_ref, s_ref, out_ref, tail_ref, ext_scr, *, pos0):
    t = pl.program_id(1)
    tb = x_ref.shape[1]
    x = x_ref[0]
    u = _rms(x, g_ref[...])

    @pl.when(t == 0)
    def _():
        ext_scr[0:POOL_HALO, :] = pre_ref[0]

    ext_scr[POOL_HALO:POOL_HALO + tb, :] = u
    pos = pos0 + t * tb + lax.broadcasted_iota(I32, (tb, 1), 0)
    ys = []
    for gi, win in enumerate(POOL_WINDOWS):
        sl = slice(gi * POOL_GROUP, (gi + 1) * POOL_GROUP)
        s = ext_scr[:, sl]
        sh = 1
        while sh < win:
            s = s + pltpu.roll(s, sh, 0)
            sh *= 2
        cnt = jnp.minimum(pos + 1, win).astype(F32)
        p = s[POOL_HALO:, :] / cnt - u[:, sl]
        ys.append(jnp.dot(p.astype(BF16), w_ref[gi], preferred_element_type=F32))
    out_ref[0] = x + jnp.concatenate(ys, axis=-1) * s_ref[...]
    tail = u[tb - POOL_HALO:, :]
    tail_ref[0] = tail
    ext_scr[0:POOL_HALO, :] = tail


def pool_mix(x, prefix, pos0, g, w_pool, s_pool, tb=256):
    b, t, d = x.shape
    tb = min(tb, t)
    return pl.pallas_call(
        functools.partial(_pool_kernel, pos0=pos0), grid=(b, t // tb),
        in_specs=[pl.BlockSpec((1, tb, d), lambda i, j: (i, j, 0)),
                  pl.BlockSpec((1, POOL_HALO, d), lambda i, j: (i, 0, 0)),
                  pl.BlockSpec((1, d), lambda i, j: (0, 0)),
                  pl.BlockSpec(w_pool.shape, lambda i, j: (0, 0, 0)),
                  pl.BlockSpec((1, d), lambda i, j: (0, 0))],
        out_specs=[pl.BlockSpec((1, tb, d), lambda i, j: (i, j, 0)),
                   pl.BlockSpec((1, POOL_HALO, d), lambda i, j: (i, 0, 0))],
        out_shape=[jax.ShapeDtypeStruct((b, t, d), F32), jax.ShapeDtypeStruct((b, POOL_HALO, d), F32)],
        scratch_shapes=[pltpu.VMEM((POOL_HALO + tb, d), F32)],
        compiler_params=_cparams(("parallel", "arbitrary")), name="pool_mix",
    )(x, prefix, g.reshape(1, d), w_pool, s_pool.reshape(1, d))


def _xattn_kernel(x_ref, k_ref, v_ref, g_ref, wq_ref, wo_ref, out_ref):
    x = x_ref[0]
    u = _rms(x, g_ref[...]).astype(BF16)
    q = jnp.dot(u, wq_ref[...], preferred_element_type=F32).astype(BF16)
    outs = []
    for h in range(HEADS):
        sl = slice(h * HEAD_DIM, (h + 1) * HEAD_DIM)
        s = lax.dot_general(q[:, sl], k_ref[0, :, sl], (((1,), (1,)), ((), ())),
                            preferred_element_type=F32) * (HEAD_DIM ** -0.5)
        e = jnp.exp(s - jnp.max(s, axis=-1, keepdims=True))
        p = e / jnp.sum(e, axis=-1, keepdims=True)
        outs.append(jnp.dot(p.astype(BF16), v_ref[0, :, sl], preferred_element_type=F32))
    o = jnp.concatenate(outs, axis=-1).astype(BF16)
    out_ref[0] = x + jnp.dot(o, wo_ref[...], preferred_element_type=F32)


def xattn(x, k, v, g, w_q, w_o, tq=512):
    b, t, d = x.shape
    tq = min(tq, t)
    xs = pl.BlockSpec((1, tq, d), lambda i, j: (i, j, 0))
    kv = pl.BlockSpec((1, N_MEM, d), lambda i, j: (i, 0, 0))
    wsp = pl.BlockSpec((d, d), lambda i, j: (0, 0))
    return pl.pallas_call(
        _xattn_kernel, grid=(b, t // tq),
        in_specs=[xs, kv, kv, pl.BlockSpec((1, d), lambda i, j: (0, 0)), wsp, wsp],
        out_specs=xs, out_shape=jax.ShapeDtypeStruct((b, t, d), F32),
        compiler_params=_cparams(("parallel", "arbitrary")), name="xattn",
    )(x, k, v, g.reshape(1, d), w_q, w_o)


def _top16_rows(problems, emits):
    neg_inf = jnp.float32(-jnp.inf)
    tb = problems[0][0][0].shape[1]
    row8 = lax.broadcasted_iota(I32, (SUBLANES, tb), 0)
    tags_of = [tags for _, tags in problems]

    def body(r, carry):
        out = []
        for vals, tags, emit in zip(carry, tags_of, emits):
            rows = [row8 + SUBLANES * i for i in range(len(vals))]
            cur = [(v, rw, None if tags is None else tg) for v, rw, tg in zip(vals, rows, tags or rows)]
            while len(cur) > 1:
                nxt = []
                for j in range(0, len(cur) - 1, 2):
                    (va, ra, ta), (vb, rb, tb_) = cur[j], cur[j + 1]
                    c = va >= vb
                    nxt.append((jnp.where(c, va, vb), jnp.where(c, ra, rb),
                                None if ta is None else jnp.where(c, ta, tb_)))
                if len(cur) % 2:
                    nxt.append(cur[-1])
                cur = nxt
            v, rw, tg = cur[0]
            for sh in (4, 2, 1):
                v2, r2 = pltpu.roll(v, sh, 0), pltpu.roll(rw, sh, 0)
                c = (v > v2) | ((v == v2) & (rw < r2))
                if tg is not None:
                    tg = jnp.where(c, tg, pltpu.roll(tg, sh, 0))
                v, rw = jnp.where(c, v, v2), jnp.where(c, rw, r2)
            emit(r, v[0:1, :], (rw if tg is None else tg)[0:1, :])
            out.append(tuple(jnp.where(rows[i] == rw, neg_inf, vals[i]) for i in range(len(vals))))
        return tuple(out)

    lax.fori_loop(0, PEER_TOPK, body, tuple(tuple(vals) for vals, _ in problems))


_N_CAND_ROWS = PEER_TOPK + 7 * SUBLANES + SUBLANES


def _route_kernel(q_ref, keys_ref, eidx_ref, gate_ref, s_scr, val_scr, idx_scr, top_scr, sel_scr):
    tb = q_ref.shape[0]
    row8 = lax.broadcasted_iota(I32, (SUBLANES, tb), 0)
    neg_inf = jnp.float32(-jnp.inf)

    for hp in range(2 * PEER_HEADS):
        s_scr[hp] = lax.dot_general(keys_ref[hp], q_ref[:, hp * PEER_HALF:(hp + 1) * PEER_HALF],
                                    (((1,), (1,)), ((), ())), preferred_element_type=F32)

    def scores(hp):
        return [s_scr[hp, i:i + SUBLANES, :] for i in range(0, PEER_N_KEYS, SUBLANES)], None

    def emit1(j):
        def emit(r, mx, pos):
            val_scr[j, pl.ds(r, 1), :] = mx
            idx_scr[j, pl.ds(r, 1), :] = pos
        return emit

    def candidates(j):
        sv0, sv1, si0, si1 = val_scr[j], val_scr[j + 1], idx_scr[j], idx_scr[j + 1]
        lo8, hi8 = slice(0, SUBLANES), slice(SUBLANES, PEER_TOPK)
        vals = [sv0[0:1, :] + sv1[lo8, :], sv0[0:1, :] + sv1[hi8, :]]
        ids = [si0[0:1, :] * PEER_N_KEYS + si1[lo8, :], si0[0:1, :] * PEER_N_KEYS + si1[hi8, :]]
        for a in range(1, SUBLANES):
            keep = row8 < PEER_TOPK // (a + 1)
            vals.append(jnp.where(keep, sv0[a:a + 1, :] + sv1[lo8, :], neg_inf))
            ids.append(si0[a:a + 1, :] * PEER_N_KEYS + si1[lo8, :])
        vals.append(sv0[hi8, :] + sv1[0:1, :])
        ids.append(si0[hi8, :] * PEER_N_KEYS + si1[0:1, :])
        return vals, ids

    def emit2(j):
        def emit(r, mx, expert):
            top_scr[j, pl.ds(r, 1), :] = mx
            sel_scr[j, pl.ds(r, 1), :] = expert
        return emit

    for h0 in range(0, PEER_HEADS, 2):
        for j in range(2):
            hp = 2 * (h0 + j)
            _top16_rows([scores(hp), scores(hp + 1)], [emit1(2 * j), emit1(2 * j + 1)])
        _top16_rows([candidates(0), candidates(2)], [emit2(0), emit2(1)])
        for j in range(2):
            top = top_scr[j]
            e = jnp.exp(top - jnp.max(top, axis=0, keepdims=True))
            rows = slice((h0 + j) * PEER_TOPK, (h0 + j + 1) * PEER_TOPK)
            gate_ref[rows, :] = e / jnp.sum(e, axis=0, keepdims=True)
            eidx_ref[rows, :] = sel_scr[j]


def peer_route(q, keys, tb=128):
    m = q.shape[0]
    out = pl.BlockSpec((PEER_SLOTS, tb), lambda i: (0, i))
    return pl.pallas_call(
        _route_kernel, grid=(m // tb,),
        in_specs=[pl.BlockSpec((tb, q.shape[1]), lambda i: (i, 0)),
                  pl.BlockSpec(keys.shape, lambda i: (0, 0, 0))],
        out_specs=[out, out],
        out_shape=[jax.ShapeDtypeStruct((PEER_SLOTS, m), I32), jax.ShapeDtypeStruct((PEER_SLOTS, m), F32)],
        scratch_shapes=[pltpu.VMEM((2 * PEER_HEADS, PEER_N_KEYS, tb), F32),
                        pltpu.VMEM((4, PEER_TOPK, tb), F32), pltpu.VMEM((4, PEER_TOPK, tb), I32),
                        pltpu.VMEM((2, PEER_TOPK, tb), F32), pltpu.VMEM((2, PEER_TOPK, tb), I32)],
        compiler_params=_cparams(("parallel",)), name="peer_route",
    )(q, keys)


GROUP = 8
SLOT_UNROLL = 32
SLOT_TRIPS = PEER_SLOTS // SLOT_UNROLL
OUT_UNROLL = 16
SPREAD_PER_TRIP = GROUP * OUT_UNROLL // PEER_SLOTS
_HI_MASK = 0xFFFF0000


def _unpack(w):
    lo = pltpu.bitcast(w << 16, F32)
    hi = pltpu.bitcast(w & jnp.uint32(_HI_MASK), F32)
    return lo, hi


def _two_rows(tab_ref, ia, ib, lo_half):
    return jnp.where(lo_half, tab_ref[pl.ds(ia, SUBLANES), :], tab_ref[pl.ds(ib - ROW_SUBLANES, SUBLANES), :])


def _transpose8(tiles, sub):
    t = list(tiles)
    for d in (4, 2, 1):
        keep = (sub & d) == 0
        for i in range(SUBLANES):
            if i & d == 0:
                a, b = t[i], t[i + d]
                t[i] = jnp.where(keep, a, pltpu.roll(b, d, 0))
                t[i + d] = jnp.where(keep, pltpu.roll(a, SUBLANES - d, 0), b)
    return t


def _split_halves(a, b, sub):
    keep = sub < ROW_SUBLANES
    return (jnp.where(keep, a, pltpu.roll(b, ROW_SUBLANES, 0)),
            jnp.where(keep, pltpu.roll(a, ROW_SUBLANES, 0), b))


def _expert_in_kernel(idx_ref, x_ref, gate_ref, tab_ref, w_ref, part_a, part_b):
    tbu = gate_ref.shape[0]
    sub = lax.broadcasted_iota(I32, (SUBLANES, LANES), 0)
    lane = lax.broadcasted_iota(I32, (SUBLANES, LANES), 1)
    lo_half = sub < ROW_SUBLANES
    m2 = (sub % 4) < 2
    m1 = (sub % 2) < 1
    pair_tokens = ((0, 4), (2, 6), (1, 5), (3, 7))

    def finish(g, accs):
        even, odd = accs[0] + accs[1], accs[2] + accs[3]
        even = even + pltpu.roll(even, SUBLANES - 1, 0)
        odd = odd + pltpu.roll(odd, SUBLANES - 1, 0)
        act = jnp.where(m1, even, pltpu.roll(odd, 1, 0))
        rows = pl.ds(pl.multiple_of(g * GROUP, GROUP), GROUP)
        gelu = 0.5 * act * (1.0 + lax.erf(act * (1.0 / math.sqrt(2.0))))
        w_ref[rows, :] = gate_ref[rows, :] * gelu

    def lane_sums(part, k, accs, j):
        at = lane == k
        for h in range(2):
            tile = part[pl.ds(pl.multiple_of((2 * k + h) * SUBLANES, SUBLANES), SUBLANES), :]
            accs[2 * h + j] = jnp.where(at, jnp.sum(tile, axis=1, keepdims=True), accs[2 * h + j])

    def group(g, part, prev):
        toks = pl.ds(pl.multiple_of(g * GROUP, GROUP), GROUP)
        per_token = _transpose8([x_ref[toks, c * LANES:(c + 1) * LANES] for c in range(SUBLANES)], sub)
        xs = []
        for ta, tb_ in pair_tokens:
            xs.extend(_split_halves(per_token[ta], per_token[tb_], sub))

        rows_of = [idx_ref.at[g * GROUP + t] for t in range(GROUP)]

        def slots(kb, accs):
            accs = list(accs)
            for kk in range(SLOT_UNROLL):
                k = kb * SLOT_UNROLL + kk
                lane_sums(prev, k, accs, kk % 2)
                f = []
                for j, (ta, tb_) in enumerate(pair_tokens):
                    ia = pl.multiple_of(rows_of[ta][k], ROW_SUBLANES)
                    ib = pl.multiple_of(rows_of[tb_][k], ROW_SUBLANES)
                    lo, hi = _unpack(_two_rows(tab_ref, ia, ib, lo_half))
                    f.append(lo * xs[2 * j] + hi * xs[2 * j + 1])
                for h, (a, b) in enumerate(((f[0], f[1]), (f[2], f[3]))):
                    part[pl.ds(pl.multiple_of((2 * k + h) * SUBLANES, SUBLANES), SUBLANES), :] = (
                        jnp.where(m2, a, pltpu.roll(b, 2, 0)) + jnp.where(m2, pltpu.roll(a, 6, 0), b))
            return tuple(accs)

        return lax.fori_loop(0, SLOT_TRIPS, slots, tuple(jnp.zeros((SUBLANES, LANES), F32) for _ in range(4)))

    part_b[...] = jnp.zeros(part_b.shape, F32)

    def group_pair(i, carry):
        accs = group(2 * i, part_a, part_b)

        @pl.when(i > 0)
        def _():
            finish(2 * i - 1, accs)

        finish(2 * i, group(2 * i + 1, part_b, part_a))
        return carry

    lax.fori_loop(0, tbu // (2 * GROUP), group_pair, 0)
    accs = [jnp.zeros((SUBLANES, LANES), F32) for _ in range(4)]
    for k in range(PEER_SLOTS):
        lane_sums(part_b, k, accs, k % 2)
    finish(tbu // GROUP - 1, accs)


def expert_in(idx4, u, gate, tab, tbu=128):
    m = gate.shape[0]
    tbu = min(tbu, m)
    return pl.pallas_call(
        _expert_in_kernel, grid=(m // tbu,),
        in_specs=[pl.BlockSpec((tbu, LANES), lambda i: (i, 0), memory_space=pltpu.SMEM),
                  pl.BlockSpec((tbu, D_MODEL), lambda i: (i, 0)),
                  pl.BlockSpec((tbu, PEER_SLOTS), lambda i: (i, 0)),
                  pl.BlockSpec(tab.shape, lambda i: (0, 0), pipeline_mode=pl.Buffered(1))],
        out_specs=pl.BlockSpec((tbu, PEER_SLOTS), lambda i: (i, 0)),
        out_shape=jax.ShapeDtypeStruct((m, PEER_SLOTS), F32),
        scratch_shapes=[pltpu.VMEM((2 * PEER_SLOTS * SUBLANES, LANES), F32)] * 2,
        compiler_params=_cparams(("arbitrary",), VMEM_LIMIT), name="expert_in",
    )(idx4, u, gate, tab)


def _expert_out_kernel(idx_ref, w_ref, x_ref, tab_ref, out_ref, wrep_a, wrep_b):
    tbv = idx_ref.shape[0]
    sub = lax.broadcasted_iota(I32, (SUBLANES, LANES), 0)
    lo_half = sub < ROW_SUBLANES
    pair_tokens = ((0, 1), (2, 3), (4, 5), (6, 7))

    def spread(wrep, tok, slot):
        col = jnp.broadcast_to(w_ref[pl.ds(tok, 1), :], (PEER_SLOTS, LANES)).T
        wrep[pl.ds(pl.multiple_of(slot * PEER_SLOTS, PEER_SLOTS), PEER_SLOTS), :] = col

    def group(g, wrep, wrep_next):
        rows_of = [idx_ref.at[g * GROUP + t] for t in range(GROUP)]

        def slots(kb, accs):
            accs = list(accs)
            for s in range(SPREAD_PER_TRIP):
                t_next = kb * SPREAD_PER_TRIP + s
                spread(wrep_next, jnp.minimum((g + 1) * GROUP + t_next, tbv - 1), t_next)
            wbase = pl.multiple_of(kb * OUT_UNROLL, OUT_UNROLL)
            for kk in range(OUT_UNROLL):
                k = kb * OUT_UNROLL + kk
                for j, (ta, tb_) in enumerate(pair_tokens):
                    ia = pl.multiple_of(rows_of[ta][k], ROW_SUBLANES)
                    ib = pl.multiple_of(rows_of[tb_][k], ROW_SUBLANES)
                    lo, hi = _unpack(_two_rows(tab_ref, ia, ib, lo_half))
                    wt = jnp.where(lo_half, wrep[pl.ds(wbase + (ta * PEER_SLOTS + kk), 1), :],
                                   wrep[pl.ds(wbase + (tb_ * PEER_SLOTS + kk), 1), :])
                    accs[2 * j] = accs[2 * j] + wt * lo
                    accs[2 * j + 1] = accs[2 * j + 1] + wt * hi
            return tuple(accs)

        accs = lax.fori_loop(0, PEER_SLOTS // OUT_UNROLL, slots,
                             tuple(jnp.zeros((SUBLANES, LANES), F32) for _ in range(GROUP)))
        per_token = [None] * GROUP
        for j, (ta, tb_) in enumerate(pair_tokens):
            per_token[ta], per_token[tb_] = _split_halves(accs[2 * j], accs[2 * j + 1], sub)
        toks = pl.ds(pl.multiple_of(g * GROUP, GROUP), GROUP)
        for c, tile in enumerate(_transpose8(per_token, sub)):
            cols = slice(c * LANES, (c + 1) * LANES)
            out_ref[toks, cols] = x_ref[toks, cols] + tile

    def group_pair(i, carry):
        group(2 * i, wrep_a, wrep_b)
        group(2 * i + 1, wrep_b, wrep_a)
        return carry

    for t in range(GROUP):
        spread(wrep_a, t, t)
    lax.fori_loop(0, tbv // (2 * GROUP), group_pair, 0)


def expert_out(idx4, w, x, tab, tbv=128):
    m = w.shape[0]
    tbv = min(tbv, m)
    rows = pl.BlockSpec((tbv, D_MODEL), lambda i: (i, 0))
    return pl.pallas_call(
        _expert_out_kernel, grid=(m // tbv,),
        in_specs=[pl.BlockSpec((tbv, PEER_SLOTS), lambda i: (i, 0), memory_space=pltpu.SMEM),
                  pl.BlockSpec((tbv, PEER_SLOTS), lambda i: (i, 0)), rows,
                  pl.BlockSpec(tab.shape, lambda i: (0, 0), pipeline_mode=pl.Buffered(1))],
        out_specs=rows, out_shape=jax.ShapeDtypeStruct(x.shape, F32),
        scratch_shapes=[pltpu.VMEM((GROUP * PEER_SLOTS, LANES), F32)] * 2,
        compiler_params=_cparams(("arbitrary",), VMEM_LIMIT), name="expert_out",
    )(idx4, w, x, tab)


def _pack_table(t):
    bits = lax.bitcast_convert_type(t.astype(BF16), jnp.uint16).astype(U32)
    packed = bits[:, :ROW_WORDS] | (bits[:, ROW_WORDS:] << 16)
    return jnp.pad(packed.reshape(t.shape[0] * ROW_SUBLANES, LANES), ((TABLE_PAD, TABLE_PAD), (0, 0)))


def peer(x, g, w_q, keys, tab_u, tab_v):
    m = x.shape[0]
    q, u = norm_matmul(x, g, [w_q], [BF16], emit_u=True)
    eidx_t, gate_t = peer_route(q, keys, tb=min(128, m))
    idx4 = eidx_t.T * ROW_SUBLANES + TABLE_PAD
    w = expert_in(idx4, u, gate_t.T, tab_u)
    return expert_out(idx4, w, x, tab_v)


def kernel(x_prompt, x_sample, state_mlstm_C, state_mlstm_n, state_mlstm_m, state_pool, cache_mem_k, cache_mem_v, mem_prompt, g_mix, w_mlstm_in, b_mlstm_i, b_mlstm_f, g_mlstm_head, w_mlstm_out, w_pool, s_pool, g_xattn, g_mem, w_xq, w_xk, w_xv, w_xo, g_ffn, w_peer_q, peer_keys, peer_u, peer_v, g_final):
    d = D_MODEL
    bp, tp, _ = x_prompt.shape
    bs, ts, _ = x_sample.shape
    depth = g_mix.shape[0]
    groups = [dict(x=x_prompt.reshape(bp * tp, d), b=bp, t=tp),
              dict(x=x_sample.reshape(bs * ts, d), b=bs, t=ts)]
    outs = [dict(C=[], n=[], m=[], pool=[]) for _ in groups]
    mk_p, mv_p = [], []
    mem = mem_prompt.reshape(bp * N_MEM, d)

    for i in range(depth):
        j = i // 2
        if i % 2 == 0:
            w_in = w_mlstm_in[j]
            w_main = w_in[:, :4 * d].astype(BF16)
            w_gate = jnp.pad(w_in[:, 4 * d:], ((0, 0), (0, LANES - 2 * HEADS)))
            bias = jnp.pad(jnp.concatenate([b_mlstm_i[j], b_mlstm_f[j]]), (0, LANES - 2 * HEADS)).reshape(1, LANES)
            w_out = w_mlstm_out[j].astype(BF16)
        else:
            w_pl = w_pool[j].astype(BF16)
        w_kv = [w_xk[i].astype(BF16), w_xv[i].astype(BF16)]
        w_q, w_o = w_xq[i].astype(BF16), w_xo[i].astype(BF16)
        w_pq = w_peer_q[i].astype(BF16)
        keys = peer_keys[i].reshape(2 * PEER_HEADS, PEER_N_KEYS, PEER_HALF).astype(BF16)
        tab_u, tab_v = _pack_table(peer_u[i]), _pack_table(peer_v[i])

        kp, vp = norm_matmul(mem, g_mem[i], w_kv, [F32, F32])
        mk_p.append(kp.reshape(bp, N_MEM, HEADS, HEAD_DIM))
        mv_p.append(vp.reshape(bp, N_MEM, HEADS, HEAD_DIM))
        kvs = [(kp.reshape(bp, N_MEM, d).astype(BF16), vp.reshape(bp, N_MEM, d).astype(BF16)),
               (cache_mem_k[i].reshape(bs, N_MEM, d).astype(BF16), cache_mem_v[i].reshape(bs, N_MEM, d).astype(BF16))]

        for gi, (grp, out) in enumerate(zip(groups, outs)):
            x, b, t = grp["x"], grp["b"], grp["t"]
            if i % 2 == 0:
                z, gz = norm_matmul(x, g_mix[i], [w_main, w_gate], [F32, F32], hi=(False, True))
                if gi == 0:
                    c0 = jnp.zeros((b, HEADS, HEAD_DIM, HEAD_DIM), F32)
                    n0 = jnp.zeros((b, HEADS, HEAD_DIM), F32)
                    m0 = jnp.zeros((b, HEADS), F32)
                else:
                    c0, n0, m0 = state_mlstm_C[j], state_mlstm_n[j], state_mlstm_m[j]
                h, c1, n1, m1 = mlstm_scan(z.reshape(b, t, 4 * d), gz.reshape(b, t, LANES), bias, c0, n0, m0)
                out["C"].append(c1); out["n"].append(n1); out["m"].append(m1)
                x = mlstm_out(h.reshape(b * t, d), z, x, g_mlstm_head[j], w_out)
            else:
                if gi == 0:
                    prefix, pos0 = jnp.zeros((b, POOL_HALO, d), F32), 0
                else:
                    prefix, pos0 = jnp.pad(state_pool[j], ((0, 0), (1, 0), (0, 0))), 4096
                x3, tail = pool_mix(x.reshape(b, t, d), prefix, pos0, g_mix[i], w_pl, s_pool[j])
                out["pool"].append(tail[:, 1:])
                x = x3.reshape(b * t, d)
            kb, vb = kvs[gi]
            x = xattn(x.reshape(b, t, d), kb, vb, g_xattn[i], w_q, w_o).reshape(b * t, d)
            x = peer(x, g_ffn[i], w_pq, keys, tab_u, tab_v)
            grp["x"] = x

    y = [rmsnorm(grp["x"], g_final).reshape(grp["b"], grp["t"], d) for grp in groups]
    st = lambda lst: jnp.stack(lst)
    po, so = outs
    return (y[0], y[1],
            st(po["C"]), st(po["n"]), st(po["m"]), st(po["pool"]), st(mk_p), st(mv_p),
            st(so["C"]), st(so["n"]), st(so["m"]), st(so["pool"]))
```

```python
import functools
import math

import jax
import jax.numpy as jnp
from jax import lax
from jax.experimental import pallas as pl
from jax.experimental.pallas import tpu as pltpu

F32 = jnp.float32
BF16 = jnp.bfloat16
I32 = jnp.int32
U32 = jnp.uint32
EPS = 1e-6
HIGHEST = lax.Precision.HIGHEST

D_MODEL = 1024
CHUNK = 64
HEADS = 4
HEAD_DIM = D_MODEL // HEADS
POOL_WINDOWS = (2, 4, 8, 16)
POOL_GROUP = D_MODEL // len(POOL_WINDOWS)
POOL_HALO = 16
N_MEM = 256
PAST_LEN = 4096
PEER_HEADS = 8
PEER_N_KEYS = 128
PEER_HALF = 128
PEER_TOPK = 16
PEER_SLOTS = PEER_HEADS * PEER_TOPK
LANES = 128
SUBLANES = 8
ROW_WORDS = D_MODEL // 2
ROW_SUBLANES = ROW_WORDS // LANES
TABLE_PAD = SUBLANES - ROW_SUBLANES
VMEM_LIMIT = 56 * 1024 * 1024


def _cparams(sem, vmem=None):
    return pltpu.CompilerParams(dimension_semantics=sem, vmem_limit_bytes=vmem)


def _rms(x, g):
    return x * lax.rsqrt(jnp.mean(x * x, axis=-1, keepdims=True) + EPS) * g


def _norm_matmul_kernel(*refs, n_w, hi, emit_u, n_chunk):
    x_ref, g_ref = refs[0], refs[1]
    w_refs = refs[2:2 + n_w]
    o_refs = refs[2 + n_w:]
    u = _rms(x_ref[...], g_ref[...])
    ub = u.astype(BF16)
    for w_ref, o_ref, h in zip(w_refs, o_refs, hi):
        n = w_ref.shape[1]
        step = min(n, n_chunk)
        for j in range(0, n, step):
            if h:
                o = jnp.dot(u, w_ref[:, j:j + step], precision=HIGHEST, preferred_element_type=F32)
            else:
                o = jnp.dot(ub, w_ref[:, j:j + step], preferred_element_type=F32)
            o_ref[:, j:j + step] = o.astype(o_ref.dtype)
    if emit_u:
        o_refs[n_w][...] = u


def norm_matmul(x, g, ws, out_dtypes, hi=None, emit_u=False, tm=512):
    m, d = x.shape
    tm = min(tm, m)
    hi = tuple(hi or (False,) * len(ws))
    in_specs = [pl.BlockSpec((tm, d), lambda i: (i, 0)), pl.BlockSpec((1, d), lambda i: (0, 0))]
    in_specs += [pl.BlockSpec(w.shape, lambda i: (0, 0)) for w in ws]
    out_shape = [jax.ShapeDtypeStruct((m, w.shape[1]), dt) for w, dt in zip(ws, out_dtypes)]
    out_specs = [pl.BlockSpec((tm, w.shape[1]), lambda i: (i, 0)) for w in ws]
    if emit_u:
        out_shape.append(jax.ShapeDtypeStruct((m, d), F32))
        out_specs.append(pl.BlockSpec((tm, d), lambda i: (i, 0)))
    return pl.pallas_call(
        functools.partial(_norm_matmul_kernel, n_w=len(ws), hi=hi, emit_u=emit_u, n_chunk=512),
        grid=(m // tm,), in_specs=in_specs, out_specs=out_specs, out_shape=out_shape,
        compiler_params=_cparams(("parallel",), VMEM_LIMIT), name="norm_matmul",
    )(x, g.reshape(1, d), *ws)


def _rmsnorm_kernel(x_ref, g_ref, o_ref):
    o_ref[...] = _rms(x_ref[...], g_ref[...])


def rmsnorm(x, g, tm=512):
    m, d = x.shape
    tm = min(tm, m)
    return pl.pallas_call(
        _rmsnorm_kernel, grid=(m // tm,),
        in_specs=[pl.BlockSpec((tm, d), lambda i: (i, 0)), pl.BlockSpec((1, d), lambda i: (0, 0))],
        out_specs=pl.BlockSpec((tm, d), lambda i: (i, 0)),
        out_shape=jax.ShapeDtypeStruct((m, d), F32),
        compiler_params=_cparams(("parallel",)), name="final_rmsnorm",
    )(x, g.reshape(1, d))


def _log_sigmoid(x):
    return jnp.minimum(x, 0.0) - jnp.log1p(jnp.exp(-jnp.abs(x)))


def _mlstm_kernel(q_ref, k_ref, v_ref, gz_ref, bias_ref, c0_ref, n0_ref, m0_ref,
                  h_ref, c_out, n_out, m_out, c_scr, n_scr, m_scr):
    c = pl.program_id(1)
    L = q_ref.shape[1]

    @pl.when(c == 0)
    def _():
        c_scr[...] = c0_ref[0]
        n_scr[...] = n0_ref[0]
        m_scr[...] = m0_ref[0]

    row = lax.broadcasted_iota(I32, (L, L), 0)
    col = lax.broadcasted_iota(I32, (L, L), 1)
    tril = col <= row
    eye = col == row
    pre = gz_ref[0] + bias_ref[...]
    neg_inf = jnp.float32(-jnp.inf)

    for h in range(HEADS):
        sl = slice(h * HEAD_DIM, (h + 1) * HEAD_DIM)
        q = q_ref[0, :, sl]
        k = k_ref[0, :, sl] * (HEAD_DIM ** -0.5)
        v = v_ref[0, :, sl]
        qb, kb, vb = q.astype(BF16), k.astype(BF16), v.astype(BF16)
        ig_c = pre[:, h:h + 1]
        lf_c = _log_sigmoid(pre[:, HEADS + h:HEADS + h + 1])
        ig_r = jnp.sum(jnp.where(eye, ig_c, 0.0), axis=0, keepdims=True)
        lf_r = jnp.sum(jnp.where(eye, lf_c, 0.0), axis=0, keepdims=True)
        bt_c = jnp.sum(jnp.where(tril, lf_r, 0.0), axis=1, keepdims=True)
        bt_r = jnp.sum(jnp.where(row <= col, lf_c, 0.0), axis=0, keepdims=True)
        m_prev = m_scr[h:h + 1, 0:1]
        logw = jnp.where(tril, bt_c - bt_r + ig_r, neg_inf)
        inter = bt_c + m_prev
        m_t = jnp.maximum(inter, jnp.max(logw, axis=1, keepdims=True))
        w = jnp.exp(logw - m_t)
        a_inter = jnp.exp(inter - m_t)
        s = lax.dot_general(qb, kb, (((1,), (1,)), ((), ())), preferred_element_type=F32)
        sw = s * w
        cm = c_scr[h]
        nm = n_scr[h:h + 1, :]
        num = a_inter * jnp.dot(qb, cm.astype(BF16), preferred_element_type=F32) \
            + jnp.dot(sw.astype(BF16), vb, preferred_element_type=F32)
        den = a_inter * jnp.sum(q * nm, axis=1, keepdims=True) + jnp.sum(sw, axis=1, keepdims=True)
        h_ref[0, :, sl] = num / jnp.maximum(jnp.abs(den), jnp.exp(-m_t))
        m_new = m_t[L - 1:L, :]
        bt_last = bt_c[L - 1:L, :]
        g_state = jnp.exp(bt_last + m_prev - m_new)
        w_s = jnp.exp(bt_last - bt_c + ig_c - m_new)
        kw = k * w_s
        c_scr[h] = g_state * cm + lax.dot_general(kw.astype(BF16), vb, (((0,), (0,)), ((), ())),
                                                   preferred_element_type=F32)
        n_scr[h:h + 1, :] = g_state * nm + jnp.sum(kw, axis=0, keepdims=True)
        m_scr[h:h + 1, :] = jnp.broadcast_to(m_new, (1, LANES))

    @pl.when(c == pl.num_programs(1) - 1)
    def _():
        c_out[0] = c_scr[...]
        n_out[0] = n_scr[...]
        m_out[0] = m_scr[...]


def mlstm_scan(z, gz, bias, c0, n0, m0):
    b, t, _ = z.shape
    nc = t // CHUNK
    blk = lambda j: pl.BlockSpec((1, CHUNK, D_MODEL), lambda i, c: (i, c, j))
    st4 = pl.BlockSpec((1, HEADS, HEAD_DIM, HEAD_DIM), lambda i, c: (i, 0, 0, 0))
    st3 = pl.BlockSpec((1, HEADS, HEAD_DIM), lambda i, c: (i, 0, 0))
    stm = pl.BlockSpec((1, HEADS, LANES), lambda i, c: (i, 0, 0))
    m0b = jnp.broadcast_to(m0[:, :, None], (b, HEADS, LANES))
    h, c_new, n_new, m_new = pl.pallas_call(
        _mlstm_kernel, grid=(b, nc),
        in_specs=[blk(0), blk(1), blk(2),
                  pl.BlockSpec((1, CHUNK, LANES), lambda i, c: (i, c, 0)),
                  pl.BlockSpec((1, LANES), lambda i, c: (0, 0)), st4, st3, stm],
        out_specs=[pl.BlockSpec((1, CHUNK, D_MODEL), lambda i, c: (i, c, 0)), st4, st3, stm],
        out_shape=[jax.ShapeDtypeStruct((b, t, D_MODEL), F32),
                   jax.ShapeDtypeStruct((b, HEADS, HEAD_DIM, HEAD_DIM), F32),
                   jax.ShapeDtypeStruct((b, HEADS, HEAD_DIM), F32),
                   jax.ShapeDtypeStruct((b, HEADS, LANES), F32)],
        scratch_shapes=[pltpu.VMEM((HEADS, HEAD_DIM, HEAD_DIM), F32),
                        pltpu.VMEM((HEADS, HEAD_DIM), F32),
                        pltpu.VMEM((HEADS, LANES), F32)],
        compiler_params=_cparams(("parallel", "arbitrary")), name="mlstm_scan",
    )(z, z, z, gz, bias, c0, n0, m0b)
    return h, c_new, n_new, m_new[:, :, 0]


def _mlstm_out_kernel(h_ref, o_ref, x_ref, g_ref, w_ref, out_ref):
    hh = h_ref[...]
    parts = []
    for i in range(HEADS):
        a = hh[:, i * HEAD_DIM:(i + 1) * HEAD_DIM]
        d = a - jnp.mean(a, axis=-1, keepdims=True)
        parts.append(d * lax.rsqrt(jnp.mean(d * d, axis=-1, keepdims=True) + EPS))
    hn = jnp.concatenate(parts, axis=-1) * g_ref[...]
    y = (hn * jax.nn.sigmoid(o_ref[...])).astype(BF16)
    out_ref[...] = x_ref[...] + jnp.dot(y, w_ref[...], preferred_element_type=F32)


def mlstm_out(h, z, x, g_head, w_out, tm=512):
    m, d = x.shape
    tm = min(tm, m)
    row = pl.BlockSpec((tm, d), lambda i: (i, 0))
    return pl.pallas_call(
        _mlstm_out_kernel, grid=(m // tm,),
        in_specs=[row, pl.BlockSpec((tm, d), lambda i: (i, 3)), row,
                  pl.BlockSpec((1, d), lambda i: (0, 0)), pl.BlockSpec((d, d), lambda i: (0, 0))],
        out_specs=row, out_shape=jax.ShapeDtypeStruct((m, d), F32),
        compiler_params=_cparams(("parallel",)), name="mlstm_out",
    )(h, z, x, g_head.reshape(1, d), w_out)


def _pool_kernel(x_ref, pre_ref, g_ref, w_ref, s_ref, out_ref, tail_ref, ext_scr, *, pos0):
    t = pl.program_id(1)
    tb = x_ref.shape[1]
    x = x_ref[0]
    u = _rms(x, g_ref[...])

    @pl.when(t == 0)
    def _():
        ext_scr[0:POOL_HALO, :] = pre_ref[0]

    ext_scr[POOL_HALO:POOL_HALO + tb, :] = u
    pos = pos0 + t * tb + lax.broadcasted_iota(I32, (tb, 1), 0)
    ys = []
    for gi, win in enumerate(POOL_WINDOWS):
        sl = slice(gi * POOL_GROUP, (gi + 1) * POOL_GROUP)
        s = ext_scr[:, sl]
        sh = 1
        while sh < win:
            s = s + pltpu.roll(s, sh, 0)
            sh *= 2
        cnt = jnp.minimum(pos + 1, win).astype(F32)
        p = s[POOL_HALO:, :] / cnt - u[:, sl]
        ys.append(jnp.dot(p.astype(BF16), w_ref[gi], preferred_element_type=F32))
    out_ref[0] = x + jnp.concatenate(ys, axis=-1) * s_ref[...]
    tail = u[tb - POOL_HALO:, :]
    tail_ref[0] = tail
    ext_scr[0:POOL_HALO, :] = tail


def pool_mix(x, prefix, pos0, g, w_pool, s_pool, tb=256):
    b, t, d = x.shape
    tb = min(tb, t)
    return pl.pallas_call(
        functools.partial(_pool_kernel, pos0=pos0), grid=(b, t // tb),
        in_specs=[pl.BlockSpec((1, tb, d), lambda i, j: (i, j, 0)),
                  pl.BlockSpec((1, POOL_HALO, d), lambda i, j: (i, 0, 0)),
                  pl.BlockSpec((1, d), lambda i, j: (0, 0)),
                  pl.BlockSpec(w_pool.shape, lambda i, j: (0, 0, 0)),
                  pl.BlockSpec((1, d), lambda i, j: (0, 0))],
        out_specs=[pl.BlockSpec((1, tb, d), lambda i, j: (i, j, 0)),
                   pl.BlockSpec((1, POOL_HALO, d), lambda i, j: (i, 0, 0))],
        out_shape=[jax.ShapeDtypeStruct((b, t, d), F32), jax.ShapeDtypeStruct((b, POOL_HALO, d), F32)],
        scratch_shapes=[pltpu.VMEM((POOL_HALO + tb, d), F32)],
        compiler_params=_cparams(("parallel", "arbitrary")), name="pool_mix",
    )(x, prefix, g.reshape(1, d), w_pool, s_pool.reshape(1, d))


def _xattn_kernel(x_ref, k_ref, v_ref, g_ref, wq_ref, wo_ref, out_ref):
    x = x_ref[0]
    u = _rms(x, g_ref[...]).astype(BF16)
    q = jnp.dot(u, wq_ref[...], preferred_element_type=F32).astype(BF16)
    outs = []
    for h in range(HEADS):
        sl = slice(h * HEAD_DIM, (h + 1) * HEAD_DIM)
        s = lax.dot_general(q[:, sl], k_ref[0, :, sl], (((1,), (1,)), ((), ())),
                            preferred_element_type=F32) * (HEAD_DIM ** -0.5)
        e = jnp.exp(s - jnp.max(s, axis=-1, keepdims=True))
        p = e / jnp.sum(e, axis=-1, keepdims=True)
        outs.append(jnp.dot(p.astype(BF16), v_ref[0, :, sl], preferred_element_type=F32))
    o = jnp.concatenate(outs, axis=-1).astype(BF16)
    out_ref[0] = x + jnp.dot(o, wo_ref[...], preferred_element_type=F32)


def xattn(x, k, v, g, w_q, w_o, tq=512):
    b, t, d = x.shape
    tq = min(tq, t)
    xs = pl.BlockSpec((1, tq, d), lambda i, j: (i, j, 0))
    kv = pl.BlockSpec((1, N_MEM, d), lambda i, j: (i, 0, 0))
    wsp = pl.BlockSpec((d, d), lambda i, j: (0, 0))
    return pl.pallas_call(
        _xattn_kernel, grid=(b, t // tq),
        in_specs=[xs, kv, kv, pl.BlockSpec((1, d), lambda i, j: (0, 0)), wsp, wsp],
        out_specs=xs, out_shape=jax.ShapeDtypeStruct((b, t, d), F32),
        compiler_params=_cparams(("parallel", "arbitrary")), name="xattn",
    )(x, k, v, g.reshape(1, d), w_q, w_o)


def _top16_rows(problems, emits):
    neg_inf = jnp.float32(-jnp.inf)
    tb = problems[0][0][0].shape[1]
    row8 = lax.broadcasted_iota(I32, (SUBLANES, tb), 0)
    tags_of = [tags for _, tags in problems]

    def body(r, carry):
        out = []
        for vals, tags, emit in zip(carry, tags_of, emits):
            rows = [row8 + SUBLANES * i for i in range(len(vals))]
            cur = [(v, rw, None if tags is None else tg) for v, rw, tg in zip(vals, rows, tags or rows)]
            while len(cur) > 1:
                nxt = []
                for j in range(0, len(cur) - 1, 2):
                    (va, ra, ta), (vb, rb, tb_) = cur[j], cur[j + 1]
                    c = va >= vb
                    nxt.append((jnp.where(c, va, vb), jnp.where(c, ra, rb),
                                None if ta is None else jnp.where(c, ta, tb_)))
                if len(cur) % 2:
                    nxt.append(cur[-1])
                cur = nxt
            v, rw, tg = cur[0]
            for sh in (4, 2, 1):
                v2, r2 = pltpu.roll(v, sh, 0), pltpu.roll(rw, sh, 0)
                c = (v > v2) | ((v == v2) & (rw < r2))
                if tg is not None:
                    tg = jnp.where(c, tg, pltpu.roll(tg, sh, 0))
                v, rw = jnp.where(c, v, v2), jnp.where(c, rw, r2)
            emit(r, v[0:1, :], (rw if tg is None else tg)[0:1, :])
            out.append(tuple(jnp.where(rows[i] == rw, neg_inf, vals[i]) for i in range(len(vals))))
        return tuple(out)

    lax.fori_loop(0, PEER_TOPK, body, tuple(tuple(vals) for vals, _ in problems))


_N_CAND_ROWS = PEER_TOPK + 7 * SUBLANES + SUBLANES


def _route_kernel(q_ref, keys_ref, eidx_ref, gate_ref, s_scr, val_scr, idx_scr, top_scr, sel_scr):
    tb = q_ref.shape[0]
    row8 = lax.broadcasted_iota(I32, (SUBLANES, tb), 0)
    neg_inf = jnp.float32(-jnp.inf)

    for hp in range(2 * PEER_HEADS):
        s_scr[hp] = lax.dot_general(keys_ref[hp], q_ref[:, hp * PEER_HALF:(hp + 1) * PEER_HALF],
                                    (((1,), (1,)), ((), ())), preferred_element_type=F32)

    def scores(hp):
        return [s_scr[hp, i:i + SUBLANES, :] for i in range(0, PEER_N_KEYS, SUBLANES)], None

    def emit1(j):
        def emit(r, mx, pos):
            val_scr[j, pl.ds(r, 1), :] = mx
            idx_scr[j, pl.ds(r, 1), :] = pos
        return emit

    def candidates(j):
        sv0, sv1, si0, si1 = val_scr[j], val_scr[j + 1], idx_scr[j], idx_scr[j + 1]
        lo8, hi8 = slice(0, SUBLANES), slice(SUBLANES, PEER_TOPK)
        vals = [sv0[0:1, :] + sv1[lo8, :], sv0[0:1, :] + sv1[hi8, :]]
        ids = [si0[0:1, :] * PEER_N_KEYS + si1[lo8, :], si0[0:1, :] * PEER_N_KEYS + si1[hi8, :]]
        for a in range(1, SUBLANES):
            keep = row8 < PEER_TOPK // (a + 1)
            vals.append(jnp.where(keep, sv0[a:a + 1, :] + sv1[lo8, :], neg_inf))
            ids.append(si0[a:a + 1, :] * PEER_N_KEYS + si1[lo8, :])
        vals.append(sv0[hi8, :] + sv1[0:1, :])
        ids.append(si0[hi8, :] * PEER_N_KEYS + si1[0:1, :])
        return vals, ids

    def emit2(j):
        def emit(r, mx, expert):
            top_scr[j, pl.ds(r, 1), :] = mx
            sel_scr[j, pl.ds(r, 1), :] = expert
        return emit

    for h0 in range(0, PEER_HEADS, 2):
        for j in range(2):
            hp = 2 * (h0 + j)
            _top16_rows([scores(hp), scores(hp + 1)], [emit1(2 * j), emit1(2 * j + 1)])
        _top16_rows([candidates(0), candidates(2)], [emit2(0), emit2(1)])
        for j in range(2):
            top = top_scr[j]
            e = jnp.exp(top - jnp.max(top, axis=0, keepdims=True))
            rows = slice((h0 + j) * PEER_TOPK, (h0 + j + 1) * PEER_TOPK)
            gate_ref[rows, :] = e / jnp.sum(e, axis=0, keepdims=True)
            eidx_ref[rows, :] = sel_scr[j]


def peer_route(q, keys, tb=128):
    m = q.shape[0]
    out = pl.BlockSpec((PEER_SLOTS, tb), lambda i: (0, i))
    return pl.pallas_call(
        _route_kernel, grid=(m // tb,),
        in_specs=[pl.BlockSpec((tb, q.shape[1]), lambda i: (i, 0)),
                  pl.BlockSpec(keys.shape, lambda i: (0, 0, 0))],
        out_specs=[out, out],
        out_shape=[jax.ShapeDtypeStruct((PEER_SLOTS, m), I32), jax.ShapeDtypeStruct((PEER_SLOTS, m), F32)],
        scratch_shapes=[pltpu.VMEM((2 * PEER_HEADS, PEER_N_KEYS, tb), F32),
                        pltpu.VMEM((4, PEER_TOPK, tb), F32), pltpu.VMEM((4, PEER_TOPK, tb), I32),
                        pltpu.VMEM((2, PEER_TOPK, tb), F32), pltpu.VMEM((2, PEER_TOPK, tb), I32)],
        compiler_params=_cparams(("parallel",)), name="peer_route",
    )(q, keys)


GROUP = 8
SLOT_UNROLL = 32
SLOT_TRIPS = PEER_SLOTS // SLOT_UNROLL
OUT_UNROLL = 16
SPREAD_PER_TRIP = GROUP * OUT_UNROLL // PEER_SLOTS
_HI_MASK = 0xFFFF0000


def _unpack(w):
    lo = pltpu.bitcast(w << 16, F32)
    hi = pltpu.bitcast(w & jnp.uint32(_HI_MASK), F32)
    return lo, hi


def _two_rows(tab_ref, ia, ib, lo_half):
    return jnp.where(lo_half, tab_ref[pl.ds(ia, SUBLANES), :], tab_ref[pl.ds(ib - ROW_SUBLANES, SUBLANES), :])


def _transpose8(tiles, sub):
    t = list(tiles)
    for d in (4, 2, 1):
        keep = (sub & d) == 0
        for i in range(SUBLANES):
            if i & d == 0:
                a, b = t[i], t[i + d]
                t[i] = jnp.where(keep, a, pltpu.roll(b, d, 0))
                t[i + d] = jnp.where(keep, pltpu.roll(a, SUBLANES - d, 0), b)
    return t


def _split_halves(a, b, sub):
    keep = sub < ROW_SUBLANES
    return (jnp.where(keep, a, pltpu.roll(b, ROW_SUBLANES, 0)),
            jnp.where(keep, pltpu.roll(a, ROW_SUBLANES, 0), b))


def _expert_in_kernel(idx_ref, x_ref, gate_ref, tab_ref, w_ref, part_a, part_b):
    tbu = gate_ref.shape[0]
    sub = lax.broadcasted_iota(I32, (SUBLANES, LANES), 0)
    lane = lax.broadcasted_iota(I32, (SUBLANES, LANES), 1)
    lo_half = sub < ROW_SUBLANES
    m2 = (sub % 4) < 2
    m1 = (sub % 2) < 1
    pair_tokens = ((0, 4), (2, 6), (1, 5), (3, 7))

    def finish(g, accs):
        even, odd = accs[0] + accs[1], accs[2] + accs[3]
        even = even + pltpu.roll(even, SUBLANES - 1, 0)
        odd = odd + pltpu.roll(odd, SUBLANES - 1, 0)
        act = jnp.where(m1, even, pltpu.roll(odd, 1, 0))
        rows = pl.ds(pl.multiple_of(g * GROUP, GROUP), GROUP)
        gelu = 0.5 * act * (1.0 + lax.erf(act * (1.0 / math.sqrt(2.0))))
        w_ref[rows, :] = gate_ref[rows, :] * gelu

    def lane_sums(part, k, accs, j):
        at = lane == k
        for h in range(2):
            tile = part[pl.ds(pl.multiple_of((2 * k + h) * SUBLANES, SUBLANES), SUBLANES), :]
            accs[2 * h + j] = jnp.where(at, jnp.sum(tile, axis=1, keepdims=True), accs[2 * h + j])

    def group(g, part, prev):
        toks = pl.ds(pl.multiple_of(g * GROUP, GROUP), GROUP)
        per_token = _transpose8([x_ref[toks, c * LANES:(c + 1) * LANES] for c in range(SUBLANES)], sub)
        xs = []
        for ta, tb_ in pair_tokens:
            xs.extend(_split_halves(per_token[ta], per_token[tb_], sub))

        rows_of = [idx_ref.at[g * GROUP + t] for t in range(GROUP)]

        def slots(kb, accs):
            accs = list(accs)
            for kk in range(SLOT_UNROLL):
                k = kb * SLOT_UNROLL + kk
                lane_sums(prev, k, accs, kk % 2)
                f = []
                for j, (ta, tb_) in enumerate(pair_tokens):
                    ia = pl.multiple_of(rows_of[ta][k], ROW_SUBLANES)
                    ib = pl.multiple_of(rows_of[tb_][k], ROW_SUBLANES)
                    lo, hi = _unpack(_two_rows(tab_ref, ia, ib, lo_half))
                    f.append(lo * xs[2 * j] + hi * xs[2 * j + 1])
                for h, (a, b) in enumerate(((f[0], f[1]), (f[2], f[3]))):
                    part[pl.ds(pl.multiple_of((2 * k + h) * SUBLANES, SUBLANES), SUBLANES), :] = (
                        jnp.where(m2, a, pltpu.roll(b, 2, 0)) + jnp.where(m2, pltpu.roll(a, 6, 0), b))
            return tuple(accs)

        return lax.fori_loop(0, SLOT_TRIPS, slots, tuple(jnp.zeros((SUBLANES, LANES), F32) for _ in range(4)))

    part_b[...] = jnp.zeros(part_b.shape, F32)

    def group_pair(i, carry):
        accs = group(2 * i, part_a, part_b)

        @pl.when(i > 0)
        def _():
            finish(2 * i - 1, accs)

        finish(2 * i, group(2 * i + 1, part_b, part_a))
        return carry

    lax.fori_loop(0, tbu // (2 * GROUP), group_pair, 0)
    accs = [jnp.zeros((SUBLANES, LANES), F32) for _ in range(4)]
    for k in range(PEER_SLOTS):
        lane_sums(part_b, k, accs, k % 2)
    finish(tbu // GROUP - 1, accs)


def expert_in(idx4, u, gate, tab, tbu=128):
    m = gate.shape[0]
    tbu = min(tbu, m)
    return pl.pallas_call(
        _expert_in_kernel, grid=(m // tbu,),
        in_specs=[pl.BlockSpec((tbu, LANES), lambda i: (i, 0), memory_space=pltpu.SMEM),
                  pl.BlockSpec((tbu, D_MODEL), lambda i: (i, 0)),
                  pl.BlockSpec((tbu, PEER_SLOTS), lambda i: (i, 0)),
                  pl.BlockSpec(tab.shape, lambda i: (0, 0), pipeline_mode=pl.Buffered(1))],
        out_specs=pl.BlockSpec((tbu, PEER_SLOTS), lambda i: (i, 0)),
        out_shape=jax.ShapeDtypeStruct((m, PEER_SLOTS), F32),
        scratch_shapes=[pltpu.VMEM((2 * PEER_SLOTS * SUBLANES, LANES), F32)] * 2,
        compiler_params=_cparams(("arbitrary",), VMEM_LIMIT), name="expert_in",
    )(idx4, u, gate, tab)


def _expert_out_kernel(idx_ref, w_ref, x_ref, tab_ref, out_ref, wrep_a, wrep_b):
    tbv = idx_ref.shape[0]
    sub = lax.broadcasted_iota(I32, (SUBLANES, LANES), 0)
    lo_half = sub < ROW_SUBLANES
    pair_tokens = ((0, 1), (2, 3), (4, 5), (6, 7))

    def spread(wrep, tok, slot):
        col = jnp.broadcast_to(w_ref[pl.ds(tok, 1), :], (PEER_SLOTS, LANES)).T
        wrep[pl.ds(pl.multiple_of(slot * PEER_SLOTS, PEER_SLOTS), PEER_SLOTS), :] = col

    def group(g, wrep, wrep_next):
        rows_of = [idx_ref.at[g * GROUP + t] for t in range(GROUP)]

        def gather_slot(k):
            return [_two_rows(tab_ref, pl.multiple_of(rows_of[ta][k], ROW_SUBLANES),
                              pl.multiple_of(rows_of[tb_][k], ROW_SUBLANES), lo_half) for ta, tb_ in pair_tokens]

        def slots(kb, carry):
            accs, first = list(carry[:GROUP]), carry[GROUP:]
            for s in range(SPREAD_PER_TRIP):
                t_next = kb * SPREAD_PER_TRIP + s
                spread(wrep_next, jnp.minimum((g + 1) * GROUP + t_next, tbv - 1), t_next)
            wbase = pl.multiple_of(kb * OUT_UNROLL, OUT_UNROLL)
            for kk in range(OUT_UNROLL):
                tiles = first if kk == 0 else gather_slot(kb * OUT_UNROLL + kk)
                for j, (ta, tb_) in enumerate(pair_tokens):
                    lo, hi = _unpack(tiles[j])
                    wt = jnp.where(lo_half, wrep[pl.ds(wbase + (ta * PEER_SLOTS + kk), 1), :],
                                   wrep[pl.ds(wbase + (tb_ * PEER_SLOTS + kk), 1), :])
                    accs[2 * j] = accs[2 * j] + wt * lo
                    accs[2 * j + 1] = accs[2 * j + 1] + wt * hi
            return (*accs, *gather_slot(jnp.minimum((kb + 1) * OUT_UNROLL, PEER_SLOTS - 1)))

        accs = lax.fori_loop(0, PEER_SLOTS // OUT_UNROLL, slots,
                             (*(jnp.zeros((SUBLANES, LANES), F32) for _ in range(GROUP)), *gather_slot(0)))[:GROUP]
        per_token = [None] * GROUP
        for j, (ta, tb_) in enumerate(pair_tokens):
            per_token[ta], per_token[tb_] = _split_halves(accs[2 * j], accs[2 * j + 1], sub)
        toks = pl.ds(pl.multiple_of(g * GROUP, GROUP), GROUP)
        for c, tile in enumerate(_transpose8(per_token, sub)):
            cols = slice(c * LANES, (c + 1) * LANES)
            out_ref[toks, cols] = x_ref[toks, cols] + tile

    def group_pair(i, carry):
        group(2 * i, wrep_a, wrep_b)
        group(2 * i + 1, wrep_b, wrep_a)
        return carry

    for t in range(GROUP):
        spread(wrep_a, t, t)
    lax.fori_loop(0, tbv // (2 * GROUP), group_pair, 0)


def expert_out(idx4, w, x, tab, tbv=128):
    m = w.shape[0]
    tbv = min(tbv, m)
    rows = pl.BlockSpec((tbv, D_MODEL), lambda i: (i, 0))
    return pl.pallas_call(
        _expert_out_kernel, grid=(m // tbv,),
        in_specs=[pl.BlockSpec((tbv, PEER_SLOTS), lambda i: (i, 0), memory_space=pltpu.SMEM),
                  pl.BlockSpec((tbv, PEER_SLOTS), lambda i: (i, 0)), rows,
                  pl.BlockSpec(tab.shape, lambda i: (0, 0), pipeline_mode=pl.Buffered(1))],
        out_specs=rows, out_shape=jax.ShapeDtypeStruct(x.shape, F32),
        scratch_shapes=[pltpu.VMEM((GROUP * PEER_SLOTS, LANES), F32)] * 2,
        compiler_params=_cparams(("arbitrary",), VMEM_LIMIT), name="expert_out",
    )(idx4, w, x, tab)


def _pack_table(t):
    bits = lax.bitcast_convert_type(t.astype(BF16), jnp.uint16).astype(U32)
    packed = bits[:, :ROW_WORDS] | (bits[:, ROW_WORDS:] << 16)
    return jnp.pad(packed.reshape(t.shape[0] * ROW_SUBLANES, LANES), ((TABLE_PAD, TABLE_PAD), (0, 0)))


def peer(x, g, w_q, keys, tab_u, tab_v):
    m = x.shape[0]
    q, u = norm_matmul(x, g, [w_q], [BF16], emit_u=True)
    eidx_t, gate_t = peer_route(q, keys, tb=min(128, m))
    idx4 = eidx_t.T * ROW_SUBLANES + TABLE_PAD
    w = expert_in(idx4, u, gate_t.T, tab_u)
    return expert_out(idx4, w, x, tab_v)


def kernel(x_prompt, x_sample, state_mlstm_C, state_mlstm_n, state_mlstm_m, state_pool, cache_mem_k, cache_mem_v, mem_prompt, g_mix, w_mlstm_in, b_mlstm_i, b_mlstm_f, g_mlstm_head, w_mlstm_out, w_pool, s_pool, g_xattn, g_mem, w_xq, w_xk, w_xv, w_xo, g_ffn, w_peer_q, peer_keys, peer_u, peer_v, g_final):
    d = D_MODEL
    bp, tp, _ = x_prompt.shape
    bs, ts, _ = x_sample.shape
    depth = g_mix.shape[0]
    groups = [dict(x=x_prompt.reshape(bp * tp, d), b=bp, t=tp),
              dict(x=x_sample.reshape(bs * ts, d), b=bs, t=ts)]
    outs = [dict(C=[], n=[], m=[], pool=[]) for _ in groups]
    mk_p, mv_p = [], []
    mem = mem_prompt.reshape(bp * N_MEM, d)

    for i in range(depth):
        j = i // 2
        if i % 2 == 0:
            w_in = w_mlstm_in[j]
            w_main = w_in[:, :4 * d].astype(BF16)
            w_gate = jnp.pad(w_in[:, 4 * d:], ((0, 0), (0, LANES - 2 * HEADS)))
            bias = jnp.pad(jnp.concatenate([b_mlstm_i[j], b_mlstm_f[j]]), (0, LANES - 2 * HEADS)).reshape(1, LANES)
            w_out = w_mlstm_out[j].astype(BF16)
        else:
            w_pl = w_pool[j].astype(BF16)
        w_kv = [w_xk[i].astype(BF16), w_xv[i].astype(BF16)]
        w_q, w_o = w_xq[i].astype(BF16), w_xo[i].astype(BF16)
        w_pq = w_peer_q[i].astype(BF16)
        keys = peer_keys[i].reshape(2 * PEER_HEADS, PEER_N_KEYS, PEER_HALF).astype(BF16)
        tab_u, tab_v = _pack_table(peer_u[i]), _pack_table(peer_v[i])

        kp, vp = norm_matmul(mem, g_mem[i], w_kv, [F32, F32])
        mk_p.append(kp.reshape(bp, N_MEM, HEADS, HEAD_DIM))
        mv_p.append(vp.reshape(bp, N_MEM, HEADS, HEAD_DIM))
        kvs = [(kp.reshape(bp, N_MEM, d).astype(BF16), vp.reshape(bp, N_MEM, d).astype(BF16)),
               (cache_mem_k[i].reshape(bs, N_MEM, d).astype(BF16), cache_mem_v[i].reshape(bs, N_MEM, d).astype(BF16))]

        for gi, (grp, out) in enumerate(zip(groups, outs)):
            x, b, t = grp["x"], grp["b"], grp["t"]
            if i % 2 == 0:
                z, gz = norm_matmul(x, g_mix[i], [w_main, w_gate], [F32, F32], hi=(False, True))
                if gi == 0:
                    c0 = jnp.zeros((b, HEADS, HEAD_DIM, HEAD_DIM), F32)
                    n0 = jnp.zeros((b, HEADS, HEAD_DIM), F32)
                    m0 = jnp.zeros((b, HEADS), F32)
                else:
                    c0, n0, m0 = state_mlstm_C[j], state_mlstm_n[j], state_mlstm_m[j]
                h, c1, n1, m1 = mlstm_scan(z.reshape(b, t, 4 * d), gz.reshape(b, t, LANES), bias, c0, n0, m0)
                out["C"].append(c1); out["n"].append(n1); out["m"].append(m1)
                x = mlstm_out(h.reshape(b * t, d), z, x, g_mlstm_head[j], w_out)
            else:
                if gi == 0:
                    prefix, pos0 = jnp.zeros((b, POOL_HALO, d), F32), 0
                else:
                    prefix, pos0 = jnp.pad(state_pool[j], ((0, 0), (1, 0), (0, 0))), PAST_LEN
                x3, tail = pool_mix(x.reshape(b, t, d), prefix, pos0, g_mix[i], w_pl, s_pool[j])
                out["pool"].append(tail[:, 1:])
                x = x3.reshape(b * t, d)
            kb, vb = kvs[gi]
            x = xattn(x.reshape(b, t, d), kb, vb, g_xattn[i], w_q, w_o).reshape(b * t, d)
            x = peer(x, g_ffn[i], w_pq, keys, tab_u, tab_v)
            grp["x"] = x

    y = [rmsnorm(grp["x"], g_final).reshape(grp["b"], grp["t"], d) for grp in groups]
    st = lambda lst: jnp.stack(lst)
    po, so = outs
    return (y[0], y[1],
            st(po["C"]), st(po["n"]), st(po["m"]), st(po["pool"]), st(mk_p), st(mv_p),
            st(so["C"]), st(so["n"]), st(so["m"]), st(so["pool"]))
```

```python
import functools
import math

import jax
import jax.numpy as jnp
from jax import lax
from jax.experimental import pallas as pl
from jax.experimental.pallas import tpu as pltpu

F32 = jnp.float32
BF16 = jnp.bfloat16
I32 = jnp.int32
U32 = jnp.uint32
EPS = 1e-6
HIGHEST = lax.Precision.HIGHEST

D_MODEL = 1024
CHUNK = 256
HEADS = 4
HEAD_DIM = D_MODEL // HEADS
POOL_WINDOWS = (2, 4, 8, 16)
POOL_GROUP = D_MODEL // len(POOL_WINDOWS)
POOL_HALO = 16
N_MEM = 256
PAST_LEN = 4096
PEER_HEADS = 8
PEER_N_KEYS = 128
PEER_HALF = 128
PEER_TOPK = 16
PEER_SLOTS = PEER_HEADS * PEER_TOPK
LANES = 128
SUBLANES = 8
ROW_WORDS = D_MODEL // 2
ROW_SUBLANES = ROW_WORDS // LANES
TABLE_PAD = SUBLANES - ROW_SUBLANES
VMEM_LIMIT = 56 * 1024 * 1024


def _cparams(sem, vmem=None):
    return pltpu.CompilerParams(dimension_semantics=sem, vmem_limit_bytes=vmem)


def _rms(x, g):
    return x * lax.rsqrt(jnp.mean(x * x, axis=-1, keepdims=True) + EPS) * g


def _norm_matmul_kernel(*refs, n_w, hi, emit_u, n_chunk):
    x_ref, g_ref = refs[0], refs[1]
    w_refs = refs[2:2 + n_w]
    o_refs = refs[2 + n_w:]
    u = _rms(x_ref[...], g_ref[...])
    ub = u.astype(BF16)
    for w_ref, o_ref, h in zip(w_refs, o_refs, hi):
        n = w_ref.shape[1]
        step = min(n, n_chunk)
        for j in range(0, n, step):
            if h:
                o = jnp.dot(u, w_ref[:, j:j + step], precision=HIGHEST, preferred_element_type=F32)
            else:
                o = jnp.dot(ub, w_ref[:, j:j + step], preferred_element_type=F32)
            o_ref[:, j:j + step] = o.astype(o_ref.dtype)
    if emit_u:
        o_refs[n_w][...] = u


def norm_matmul(x, g, ws, out_dtypes, hi=None, emit_u=False, tm=512):
    m, d = x.shape
    tm = min(tm, m)
    hi = tuple(hi or (False,) * len(ws))
    in_specs = [pl.BlockSpec((tm, d), lambda i: (i, 0)), pl.BlockSpec((1, d), lambda i: (0, 0))]
    in_specs += [pl.BlockSpec(w.shape, lambda i: (0, 0)) for w in ws]
    out_shape = [jax.ShapeDtypeStruct((m, w.shape[1]), dt) for w, dt in zip(ws, out_dtypes)]
    out_specs = [pl.BlockSpec((tm, w.shape[1]), lambda i: (i, 0)) for w in ws]
    if emit_u:
        out_shape.append(jax.ShapeDtypeStruct((m, d), F32))
        out_specs.append(pl.BlockSpec((tm, d), lambda i: (i, 0)))
    return pl.pallas_call(
        functools.partial(_norm_matmul_kernel, n_w=len(ws), hi=hi, emit_u=emit_u, n_chunk=512),
        grid=(m // tm,), in_specs=in_specs, out_specs=out_specs, out_shape=out_shape,
        compiler_params=_cparams(("parallel",), VMEM_LIMIT), name="norm_matmul",
    )(x, g.reshape(1, d), *ws)


def _rmsnorm_kernel(x_ref, g_ref, o_ref):
    o_ref[...] = _rms(x_ref[...], g_ref[...])


def rmsnorm(x, g, tm=512):
    m, d = x.shape
    tm = min(tm, m)
    return pl.pallas_call(
        _rmsnorm_kernel, grid=(m // tm,),
        in_specs=[pl.BlockSpec((tm, d), lambda i: (i, 0)), pl.BlockSpec((1, d), lambda i: (0, 0))],
        out_specs=pl.BlockSpec((tm, d), lambda i: (i, 0)),
        out_shape=jax.ShapeDtypeStruct((m, d), F32),
        compiler_params=_cparams(("parallel",)), name="final_rmsnorm",
    )(x, g.reshape(1, d))


def _log_sigmoid(x):
    return jnp.minimum(x, 0.0) - jnp.log1p(jnp.exp(-jnp.abs(x)))


def _mlstm_kernel(q_ref, k_ref, v_ref, gz_ref, bias_ref, c0_ref, n0_ref, m0_ref,
                  h_ref, c_out, n_out, m_out, c_scr, n_scr, m_scr):
    c = pl.program_id(1)
    L = q_ref.shape[1]

    @pl.when(c == 0)
    def _():
        c_scr[...] = c0_ref[0]
        n_scr[...] = n0_ref[0]
        m_scr[...] = m0_ref[0]

    row = lax.broadcasted_iota(I32, (L, L), 0)
    col = lax.broadcasted_iota(I32, (L, L), 1)
    tril = col <= row
    eye = col == row
    pre = gz_ref[0] + bias_ref[...]
    neg_inf = jnp.float32(-jnp.inf)

    for h in range(HEADS):
        sl = slice(h * HEAD_DIM, (h + 1) * HEAD_DIM)
        q = q_ref[0, :, sl]
        k = k_ref[0, :, sl] * (HEAD_DIM ** -0.5)
        v = v_ref[0, :, sl]
        qb, kb, vb = q.astype(BF16), k.astype(BF16), v.astype(BF16)
        ig_c = pre[:, h:h + 1]
        lf_c = _log_sigmoid(pre[:, HEADS + h:HEADS + h + 1])
        ig_r = jnp.sum(jnp.where(eye, ig_c, 0.0), axis=0, keepdims=True)
        lf_r = jnp.sum(jnp.where(eye, lf_c, 0.0), axis=0, keepdims=True)
        bt_c = jnp.sum(jnp.where(tril, lf_r, 0.0), axis=1, keepdims=True)
        bt_r = jnp.sum(jnp.where(row <= col, lf_c, 0.0), axis=0, keepdims=True)
        m_prev = m_scr[h:h + 1, 0:1]
        logw = jnp.where(tril, bt_c - bt_r + ig_r, neg_inf)
        inter = bt_c + m_prev
        m_t = jnp.maximum(inter, jnp.max(logw, axis=1, keepdims=True))
        w = jnp.exp(logw - m_t)
        a_inter = jnp.exp(inter - m_t)
        s = lax.dot_general(qb, kb, (((1,), (1,)), ((), ())), preferred_element_type=F32)
        sw = s * w
        cm = c_scr[h]
        nm = n_scr[h:h + 1, :]
        num = a_inter * jnp.dot(qb, cm.astype(BF16), preferred_element_type=F32) \
            + jnp.dot(sw.astype(BF16), vb, preferred_element_type=F32)
        den = a_inter * jnp.sum(q * nm, axis=1, keepdims=True) + jnp.sum(sw, axis=1, keepdims=True)
        h_ref[0, :, sl] = num / jnp.maximum(jnp.abs(den), jnp.exp(-m_t))
        m_new = m_t[L - 1:L, :]
        bt_last = bt_c[L - 1:L, :]
        g_state = jnp.exp(bt_last + m_prev - m_new)
        w_s = jnp.exp(bt_last - bt_c + ig_c - m_new)
        kw = k * w_s
        c_scr[h] = g_state * cm + lax.dot_general(kw.astype(BF16), vb, (((0,), (0,)), ((), ())),
                                                   preferred_element_type=F32)
        n_scr[h:h + 1, :] = g_state * nm + jnp.sum(kw, axis=0, keepdims=True)
        m_scr[h:h + 1, :] = jnp.broadcast_to(m_new, (1, LANES))

    @pl.when(c == pl.num_programs(1) - 1)
    def _():
        c_out[0] = c_scr[...]
        n_out[0] = n_scr[...]
        m_out[0] = m_scr[...]


def mlstm_scan(z, gz, bias, c0, n0, m0):
    b, t, _ = z.shape
    chunk = min(CHUNK, t)
    nc = t // chunk
    blk = lambda j: pl.BlockSpec((1, chunk, D_MODEL), lambda i, c: (i, c, j))
    st4 = pl.BlockSpec((1, HEADS, HEAD_DIM, HEAD_DIM), lambda i, c: (i, 0, 0, 0))
    st3 = pl.BlockSpec((1, HEADS, HEAD_DIM), lambda i, c: (i, 0, 0))
    stm = pl.BlockSpec((1, HEADS, LANES), lambda i, c: (i, 0, 0))
    m0b = jnp.broadcast_to(m0[:, :, None], (b, HEADS, LANES))
    h, c_new, n_new, m_new = pl.pallas_call(
        _mlstm_kernel, grid=(b, nc),
        in_specs=[blk(0), blk(1), blk(2),
                  pl.BlockSpec((1, chunk, LANES), lambda i, c: (i, c, 0)),
                  pl.BlockSpec((1, LANES), lambda i, c: (0, 0)), st4, st3, stm],
        out_specs=[pl.BlockSpec((1, chunk, D_MODEL), lambda i, c: (i, c, 0)), st4, st3, stm],
        out_shape=[jax.ShapeDtypeStruct((b, t, D_MODEL), F32),
                   jax.ShapeDtypeStruct((b, HEADS, HEAD_DIM, HEAD_DIM), F32),
                   jax.ShapeDtypeStruct((b, HEADS, HEAD_DIM), F32),
                   jax.ShapeDtypeStruct((b, HEADS, LANES), F32)],
        scratch_shapes=[pltpu.VMEM((HEADS, HEAD_DIM, HEAD_DIM), F32),
                        pltpu.VMEM((HEADS, HEAD_DIM), F32),
                        pltpu.VMEM((HEADS, LANES), F32)],
        compiler_params=_cparams(("parallel", "arbitrary")), name="mlstm_scan",
    )(z, z, z, gz, bias, c0, n0, m0b)
    return h, c_new, n_new, m_new[:, :, 0]


def _mlstm_out_kernel(h_ref, o_ref, x_ref, g_ref, w_ref, out_ref):
    hh = h_ref[...]
    parts = []
    for i in range(HEADS):
        a = hh[:, i * HEAD_DIM:(i + 1) * HEAD_DIM]
        d = a - jnp.mean(a, axis=-1, keepdims=True)
        parts.append(d * lax.rsqrt(jnp.mean(d * d, axis=-1, keepdims=True) + EPS))
    hn = jnp.concatenate(parts, axis=-1) * g_ref[...]
    y = (hn * jax.nn.sigmoid(o_ref[...])).astype(BF16)
    out_ref[...] = x_ref[...] + jnp.dot(y, w_ref[...], preferred_element_type=F32)


def mlstm_out(h, z, x, g_head, w_out, tm=512):
    m, d = x.shape
    tm = min(tm, m)
    row = pl.BlockSpec((tm, d), lambda i: (i, 0))
    return pl.pallas_call(
        _mlstm_out_kernel, grid=(m // tm,),
        in_specs=[row, pl.BlockSpec((tm, d), lambda i: (i, 3)), row,
                  pl.BlockSpec((1, d), lambda i: (0, 0)), pl.BlockSpec((d, d), lambda i: (0, 0))],
        out_specs=row, out_shape=jax.ShapeDtypeStruct((m, d), F32),
        compiler_params=_cparams(("parallel",)), name="mlstm_out",
    )(h, z, x, g_head.reshape(1, d), w_out)


def _pool_kernel(x_ref, pre_ref, g_ref, w_ref, s_ref, out_ref, tail_ref, ext_scr, *, pos0):
    t = pl.program_id(1)
    tb = x_ref.shape[1]
    x = x_ref[0]
    u = _rms(x, g_ref[...])

    @pl.when(t == 0)
    def _():
        ext_scr[0:POOL_HALO, :] = pre_ref[0]

    ext_scr[POOL_HALO:POOL_HALO + tb, :] = u
    pos = pos0 + t * tb + lax.broadcasted_iota(I32, (tb, 1), 0)
    ys = []
    for gi, win in enumerate(POOL_WINDOWS):
        sl = slice(gi * POOL_GROUP, (gi + 1) * POOL_GROUP)
        s = ext_scr[:, sl]
        sh = 1
        while sh < win:
            s = s + pltpu.roll(s, sh, 0)
            sh *= 2
        cnt = jnp.minimum(pos + 1, win).astype(F32)
        p = s[POOL_HALO:, :] / cnt - u[:, sl]
        ys.append(jnp.dot(p.astype(BF16), w_ref[gi], preferred_element_type=F32))
    out_ref[0] = x + jnp.concatenate(ys, axis=-1) * s_ref[...]
    tail = u[tb - POOL_HALO:, :]
    tail_ref[0] = tail
    ext_scr[0:POOL_HALO, :] = tail


def pool_mix(x, prefix, pos0, g, w_pool, s_pool, tb=256):
    b, t, d = x.shape
    tb = min(tb, t)
    return pl.pallas_call(
        functools.partial(_pool_kernel, pos0=pos0), grid=(b, t // tb),
        in_specs=[pl.BlockSpec((1, tb, d), lambda i, j: (i, j, 0)),
                  pl.BlockSpec((1, POOL_HALO, d), lambda i, j: (i, 0, 0)),
                  pl.BlockSpec((1, d), lambda i, j: (0, 0)),
                  pl.BlockSpec(w_pool.shape, lambda i, j: (0, 0, 0)),
                  pl.BlockSpec((1, d), lambda i, j: (0, 0))],
        out_specs=[pl.BlockSpec((1, tb, d), lambda i, j: (i, j, 0)),
                   pl.BlockSpec((1, POOL_HALO, d), lambda i, j: (i, 0, 0))],
        out_shape=[jax.ShapeDtypeStruct((b, t, d), F32), jax.ShapeDtypeStruct((b, POOL_HALO, d), F32)],
        scratch_shapes=[pltpu.VMEM((POOL_HALO + tb, d), F32)],
        compiler_params=_cparams(("parallel", "arbitrary")), name="pool_mix",
    )(x, prefix, g.reshape(1, d), w_pool, s_pool.reshape(1, d))


def _xattn_kernel(x_ref, k_ref, v_ref, g_ref, wq_ref, wo_ref, out_ref):
    x = x_ref[0]
    u = _rms(x, g_ref[...]).astype(BF16)
    q = jnp.dot(u, wq_ref[...], preferred_element_type=F32).astype(BF16)
    outs = []
    for h in range(HEADS):
        sl = slice(h * HEAD_DIM, (h + 1) * HEAD_DIM)
        s = lax.dot_general(q[:, sl], k_ref[0, :, sl], (((1,), (1,)), ((), ())),
                            preferred_element_type=F32) * (HEAD_DIM ** -0.5)
        e = jnp.exp(s - jnp.max(s, axis=-1, keepdims=True))
        p = e / jnp.sum(e, axis=-1, keepdims=True)
        outs.append(jnp.dot(p.astype(BF16), v_ref[0, :, sl], preferred_element_type=F32))
    o = jnp.concatenate(outs, axis=-1).astype(BF16)
    out_ref[0] = x + jnp.dot(o, wo_ref[...], preferred_element_type=F32)


def xattn(x, k, v, g, w_q, w_o, tq=512):
    b, t, d = x.shape
    tq = min(tq, t)
    xs = pl.BlockSpec((1, tq, d), lambda i, j: (i, j, 0))
    kv = pl.BlockSpec((1, N_MEM, d), lambda i, j: (i, 0, 0))
    wsp = pl.BlockSpec((d, d), lambda i, j: (0, 0))
    return pl.pallas_call(
        _xattn_kernel, grid=(b, t // tq),
        in_specs=[xs, kv, kv, pl.BlockSpec((1, d), lambda i, j: (0, 0)), wsp, wsp],
        out_specs=xs, out_shape=jax.ShapeDtypeStruct((b, t, d), F32),
        compiler_params=_cparams(("parallel", "arbitrary")), name="xattn",
    )(x, k, v, g.reshape(1, d), w_q, w_o)


def _top16_rows(problems, emits):
    neg_inf = jnp.float32(-jnp.inf)
    tb = problems[0][0][0].shape[1]
    row8 = lax.broadcasted_iota(I32, (SUBLANES, tb), 0)
    tags_of = [tags for _, tags in problems]

    def body(r, carry):
        out = []
        for vals, tags, emit in zip(carry, tags_of, emits):
            rows = [row8 + SUBLANES * i for i in range(len(vals))]
            cur = [(v, rw, None if tags is None else tg) for v, rw, tg in zip(vals, rows, tags or rows)]
            while len(cur) > 1:
                nxt = []
                for j in range(0, len(cur) - 1, 2):
                    (va, ra, ta), (vb, rb, tb_) = cur[j], cur[j + 1]
                    c = va >= vb
                    nxt.append((jnp.where(c, va, vb), jnp.where(c, ra, rb),
                                None if ta is None else jnp.where(c, ta, tb_)))
                if len(cur) % 2:
                    nxt.append(cur[-1])
                cur = nxt
            v, rw, tg = cur[0]
            for sh in (4, 2, 1):
                v2, r2 = pltpu.roll(v, sh, 0), pltpu.roll(rw, sh, 0)
                c = (v > v2) | ((v == v2) & (rw < r2))
                if tg is not None:
                    tg = jnp.where(c, tg, pltpu.roll(tg, sh, 0))
                v, rw = jnp.where(c, v, v2), jnp.where(c, rw, r2)
            emit(r, v[0:1, :], (rw if tg is None else tg)[0:1, :])
            out.append(tuple(jnp.where(rows[i] == rw, neg_inf, vals[i]) for i in range(len(vals))))
        return tuple(out)

    lax.fori_loop(0, PEER_TOPK, body, tuple(tuple(vals) for vals, _ in problems))


_N_CAND_ROWS = PEER_TOPK + 7 * SUBLANES + SUBLANES


def _route_kernel(q_ref, keys_ref, eidx_ref, gate_ref, s_scr, val_scr, idx_scr, top_scr, sel_scr):
    tb = q_ref.shape[0]
    row8 = lax.broadcasted_iota(I32, (SUBLANES, tb), 0)
    neg_inf = jnp.float32(-jnp.inf)

    for hp in range(2 * PEER_HEADS):
        s_scr[hp] = lax.dot_general(keys_ref[hp], q_ref[:, hp * PEER_HALF:(hp + 1) * PEER_HALF],
                                    (((1,), (1,)), ((), ())), preferred_element_type=F32)

    def scores(hp):
        return [s_scr[hp, i:i + SUBLANES, :] for i in range(0, PEER_N_KEYS, SUBLANES)], None

    def emit1(j):
        def emit(r, mx, pos):
            val_scr[j, pl.ds(r, 1), :] = mx
            idx_scr[j, pl.ds(r, 1), :] = pos
        return emit

    def candidates(j):
        sv0, sv1, si0, si1 = val_scr[j], val_scr[j + 1], idx_scr[j], idx_scr[j + 1]
        lo8, hi8 = slice(0, SUBLANES), slice(SUBLANES, PEER_TOPK)
        vals = [sv0[0:1, :] + sv1[lo8, :], sv0[0:1, :] + sv1[hi8, :]]
        ids = [si0[0:1, :] * PEER_N_KEYS + si1[lo8, :], si0[0:1, :] * PEER_N_KEYS + si1[hi8, :]]
        for a in range(1, SUBLANES):
            keep = row8 < PEER_TOPK // (a + 1)
            vals.append(jnp.where(keep, sv0[a:a + 1, :] + sv1[lo8, :], neg_inf))
            ids.append(si0[a:a + 1, :] * PEER_N_KEYS + si1[lo8, :])
        vals.append(sv0[hi8, :] + sv1[0:1, :])
        ids.append(si0[hi8, :] * PEER_N_KEYS + si1[0:1, :])
        return vals, ids

    def emit2(j):
        def emit(r, mx, expert):
            top_scr[j, pl.ds(r, 1), :] = mx
            sel_scr[j, pl.ds(r, 1), :] = expert
        return emit

    for h0 in range(0, PEER_HEADS, 2):
        for j in range(2):
            hp = 2 * (h0 + j)
            _top16_rows([scores(hp), scores(hp + 1)], [emit1(2 * j), emit1(2 * j + 1)])
        _top16_rows([candidates(0), candidates(2)], [emit2(0), emit2(1)])
        for j in range(2):
            top = top_scr[j]
            e = jnp.exp(top - jnp.max(top, axis=0, keepdims=True))
            rows = slice((h0 + j) * PEER_TOPK, (h0 + j + 1) * PEER_TOPK)
            gate_ref[rows, :] = e / jnp.sum(e, axis=0, keepdims=True)
            eidx_ref[rows, :] = sel_scr[j]


def peer_route(q, keys, tb=128):
    m = q.shape[0]
    out = pl.BlockSpec((PEER_SLOTS, tb), lambda i: (0, i))
    return pl.pallas_call(
        _route_kernel, grid=(m // tb,),
        in_specs=[pl.BlockSpec((tb, q.shape[1]), lambda i: (i, 0)),
                  pl.BlockSpec(keys.shape, lambda i: (0, 0, 0))],
        out_specs=[out, out],
        out_shape=[jax.ShapeDtypeStruct((PEER_SLOTS, m), I32), jax.ShapeDtypeStruct((PEER_SLOTS, m), F32)],
        scratch_shapes=[pltpu.VMEM((2 * PEER_HEADS, PEER_N_KEYS, tb), F32),
                        pltpu.VMEM((4, PEER_TOPK, tb), F32), pltpu.VMEM((4, PEER_TOPK, tb), I32),
                        pltpu.VMEM((2, PEER_TOPK, tb), F32), pltpu.VMEM((2, PEER_TOPK, tb), I32)],
        compiler_params=_cparams(("parallel",)), name="peer_route",
    )(q, keys)


GROUP = 8
SLOT_UNROLL = 32
SLOT_TRIPS = PEER_SLOTS // SLOT_UNROLL
OUT_UNROLL = 16
SPREAD_PER_TRIP = GROUP * OUT_UNROLL // PEER_SLOTS
_HI_MASK = 0xFFFF0000


def _unpack(w):
    lo = pltpu.bitcast(w << 16, F32)
    hi = pltpu.bitcast(w & jnp.uint32(_HI_MASK), F32)
    return lo, hi


def _two_rows(tab_ref, ia, ib, lo_half):
    return jnp.where(lo_half, tab_ref[pl.ds(ia, SUBLANES), :], tab_ref[pl.ds(ib - ROW_SUBLANES, SUBLANES), :])


def _transpose8(tiles, sub):
    t = list(tiles)
    for d in (4, 2, 1):
        keep = (sub & d) == 0
        for i in range(SUBLANES):
            if i & d == 0:
                a, b = t[i], t[i + d]
                t[i] = jnp.where(keep, a, pltpu.roll(b, d, 0))
                t[i + d] = jnp.where(keep, pltpu.roll(a, SUBLANES - d, 0), b)
    return t


def _split_halves(a, b, sub):
    keep = sub < ROW_SUBLANES
    return (jnp.where(keep, a, pltpu.roll(b, ROW_SUBLANES, 0)),
            jnp.where(keep, pltpu.roll(a, ROW_SUBLANES, 0), b))


def _expert_in_kernel(idx_ref, x_ref, gate_ref, tab_ref, w_ref, part_a, part_b):
    tbu = gate_ref.shape[0]
    sub = lax.broadcasted_iota(I32, (SUBLANES, LANES), 0)
    lane = lax.broadcasted_iota(I32, (SUBLANES, LANES), 1)
    lo_half = sub < ROW_SUBLANES
    m2 = (sub % 4) < 2
    m1 = (sub % 2) < 1
    pair_tokens = ((0, 4), (2, 6), (1, 5), (3, 7))

    def finish(g, accs):
        even, odd = accs[0] + accs[1], accs[2] + accs[3]
        even = even + pltpu.roll(even, SUBLANES - 1, 0)
        odd = odd + pltpu.roll(odd, SUBLANES - 1, 0)
        act = jnp.where(m1, even, pltpu.roll(odd, 1, 0))
        rows = pl.ds(pl.multiple_of(g * GROUP, GROUP), GROUP)
        gelu = 0.5 * act * (1.0 + lax.erf(act * (1.0 / math.sqrt(2.0))))
        w_ref[rows, :] = gate_ref[rows, :] * gelu

    def lane_sums(part, k, accs, j):
        at = lane == k
        for h in range(2):
            tile = part[pl.ds(pl.multiple_of((2 * k + h) * SUBLANES, SUBLANES), SUBLANES), :]
            accs[2 * h + j] = jnp.where(at, jnp.sum(tile, axis=1, keepdims=True), accs[2 * h + j])

    def group(g, part, prev):
        toks = pl.ds(pl.multiple_of(g * GROUP, GROUP), GROUP)
        per_token = _transpose8([x_ref[toks, c * LANES:(c + 1) * LANES] for c in range(SUBLANES)], sub)
        xs = []
        for ta, tb_ in pair_tokens:
            xs.extend(_split_halves(per_token[ta], per_token[tb_], sub))

        rows_of = [idx_ref.at[g * GROUP + t] for t in range(GROUP)]

        def slots(kb, accs):
            accs = list(accs)
            for kk in range(SLOT_UNROLL):
                k = kb * SLOT_UNROLL + kk
                lane_sums(prev, k, accs, kk % 2)
                f = []
                for j, (ta, tb_) in enumerate(pair_tokens):
                    ia = pl.multiple_of(rows_of[ta][k], ROW_SUBLANES)
                    ib = pl.multiple_of(rows_of[tb_][k], ROW_SUBLANES)
                    lo, hi = _unpack(_two_rows(tab_ref, ia, ib, lo_half))
                    f.append(lo * xs[2 * j] + hi * xs[2 * j + 1])
                for h, (a, b) in enumerate(((f[0], f[1]), (f[2], f[3]))):
                    part[pl.ds(pl.multiple_of((2 * k + h) * SUBLANES, SUBLANES), SUBLANES), :] = (
                        jnp.where(m2, a, pltpu.roll(b, 2, 0)) + jnp.where(m2, pltpu.roll(a, 6, 0), b))
            return tuple(accs)

        return lax.fori_loop(0, SLOT_TRIPS, slots, tuple(jnp.zeros((SUBLANES, LANES), F32) for _ in range(4)))

    part_b[...] = jnp.zeros(part_b.shape, F32)

    def group_pair(i, carry):
        accs = group(2 * i, part_a, part_b)

        @pl.when(i > 0)
        def _():
            finish(2 * i - 1, accs)

        finish(2 * i, group(2 * i + 1, part_b, part_a))
        return carry

    lax.fori_loop(0, tbu // (2 * GROUP), group_pair, 0)
    accs = [jnp.zeros((SUBLANES, LANES), F32) for _ in range(4)]
    for k in range(PEER_SLOTS):
        lane_sums(part_b, k, accs, k % 2)
    finish(tbu // GROUP - 1, accs)


def expert_in(idx4, u, gate, tab, tbu=128):
    m = gate.shape[0]
    tbu = min(tbu, m)
    return pl.pallas_call(
        _expert_in_kernel, grid=(m // tbu,),
        in_specs=[pl.BlockSpec((tbu, LANES), lambda i: (i, 0), memory_space=pltpu.SMEM),
                  pl.BlockSpec((tbu, D_MODEL), lambda i: (i, 0)),
                  pl.BlockSpec((tbu, PEER_SLOTS), lambda i: (i, 0)),
                  pl.BlockSpec(tab.shape, lambda i: (0, 0), pipeline_mode=pl.Buffered(1))],
        out_specs=pl.BlockSpec((tbu, PEER_SLOTS), lambda i: (i, 0)),
        out_shape=jax.ShapeDtypeStruct((m, PEER_SLOTS), F32),
        scratch_shapes=[pltpu.VMEM((2 * PEER_SLOTS * SUBLANES, LANES), F32)] * 2,
        compiler_params=_cparams(("arbitrary",), VMEM_LIMIT), name="expert_in",
    )(idx4, u, gate, tab)


def _expert_out_kernel(idx_ref, w_ref, x_ref, tab_ref, out_ref, wrep_a, wrep_b):
    tbv = idx_ref.shape[0]
    sub = lax.broadcasted_iota(I32, (SUBLANES, LANES), 0)
    lo_half = sub < ROW_SUBLANES
    pair_tokens = ((0, 1), (2, 3), (4, 5), (6, 7))

    def spread(wrep, tok, slot):
        col = jnp.broadcast_to(w_ref[pl.ds(tok, 1), :], (PEER_SLOTS, LANES)).T
        wrep[pl.ds(pl.multiple_of(slot * PEER_SLOTS, PEER_SLOTS), PEER_SLOTS), :] = col

    def group(g, wrep, wrep_next):
        rows_of = [idx_ref.at[g * GROUP + t] for t in range(GROUP)]

        def gather_slot(k):
            return [_two_rows(tab_ref, pl.multiple_of(rows_of[ta][k], ROW_SUBLANES),
                              pl.multiple_of(rows_of[tb_][k], ROW_SUBLANES), lo_half) for ta, tb_ in pair_tokens]

        def slots(kb, carry):
            accs, first = list(carry[:GROUP]), carry[GROUP:]
            for s in range(SPREAD_PER_TRIP):
                t_next = kb * SPREAD_PER_TRIP + s
                spread(wrep_next, jnp.minimum((g + 1) * GROUP + t_next, tbv - 1), t_next)
            wbase = pl.multiple_of(kb * OUT_UNROLL, OUT_UNROLL)
            for kk in range(OUT_UNROLL):
                tiles = first if kk == 0 else gather_slot(kb * OUT_UNROLL + kk)
                for j, (ta, tb_) in enumerate(pair_tokens):
                    lo, hi = _unpack(tiles[j])
                    wt = jnp.where(lo_half, wrep[pl.ds(wbase + (ta * PEER_SLOTS + kk), 1), :],
                                   wrep[pl.ds(wbase + (tb_ * PEER_SLOTS + kk), 1), :])
                    accs[2 * j] = accs[2 * j] + wt * lo
                    accs[2 * j + 1] = accs[2 * j + 1] + wt * hi
            return (*accs, *gather_slot(jnp.minimum((kb + 1) * OUT_UNROLL, PEER_SLOTS - 1)))

        accs = lax.fori_loop(0, PEER_SLOTS // OUT_UNROLL, slots,
                             (*(jnp.zeros((SUBLANES, LANES), F32) for _ in range(GROUP)), *gather_slot(0)))[:GROUP]
        per_token = [None] * GROUP
        for j, (ta, tb_) in enumerate(pair_tokens):
            per_token[ta], per_token[tb_] = _split_halves(accs[2 * j], accs[2 * j + 1], sub)
        toks = pl.ds(pl.multiple_of(g * GROUP, GROUP), GROUP)
        for c, tile in enumerate(_transpose8(per_token, sub)):
            cols = slice(c * LANES, (c + 1) * LANES)
            out_ref[toks, cols] = x_ref[toks, cols] + tile

    def group_pair(i, carry):
        group(2 * i, wrep_a, wrep_b)
        group(2 * i + 1, wrep_b, wrep_a)
        return carry

    for t in range(GROUP):
        spread(wrep_a, t, t)
    lax.fori_loop(0, tbv // (2 * GROUP), group_pair, 0)


def expert_out(idx4, w, x, tab, tbv=128):
    m = w.shape[0]
    tbv = min(tbv, m)
    rows = pl.BlockSpec((tbv, D_MODEL), lambda i: (i, 0))
    return pl.pallas_call(
        _expert_out_kernel, grid=(m // tbv,),
        in_specs=[pl.BlockSpec((tbv, PEER_SLOTS), lambda i: (i, 0), memory_space=pltpu.SMEM),
                  pl.BlockSpec((tbv, PEER_SLOTS), lambda i: (i, 0)), rows,
                  pl.BlockSpec(tab.shape, lambda i: (0, 0), pipeline_mode=pl.Buffered(1))],
        out_specs=rows, out_shape=jax.ShapeDtypeStruct(x.shape, F32),
        scratch_shapes=[pltpu.VMEM((GROUP * PEER_SLOTS, LANES), F32)] * 2,
        compiler_params=_cparams(("arbitrary",), VMEM_LIMIT), name="expert_out",
    )(idx4, w, x, tab)


def _pack_table(t):
    bits = lax.bitcast_convert_type(t.astype(BF16), jnp.uint16).astype(U32)
    packed = bits[:, :ROW_WORDS] | (bits[:, ROW_WORDS:] << 16)
    return jnp.pad(packed.reshape(t.shape[0] * ROW_SUBLANES, LANES), ((TABLE_PAD, TABLE_PAD), (0, 0)))


def peer(x, g, w_q, keys, tab_u, tab_v):
    m = x.shape[0]
    q, u = norm_matmul(x, g, [w_q], [BF16], emit_u=True)
    eidx_t, gate_t = peer_route(q, keys, tb=min(128, m))
    idx4 = eidx_t.T * ROW_SUBLANES + TABLE_PAD
    w = expert_in(idx4, u, gate_t.T, tab_u)
    return expert_out(idx4, w, x, tab_v)


def kernel(x_prompt, x_sample, state_mlstm_C, state_mlstm_n, state_mlstm_m, state_pool, cache_mem_k, cache_mem_v, mem_prompt, g_mix, w_mlstm_in, b_mlstm_i, b_mlstm_f, g_mlstm_head, w_mlstm_out, w_pool, s_pool, g_xattn, g_mem, w_xq, w_xk, w_xv, w_xo, g_ffn, w_peer_q, peer_keys, peer_u, peer_v, g_final):
    d = D_MODEL
    bp, tp, _ = x_prompt.shape
    bs, ts, _ = x_sample.shape
    depth = g_mix.shape[0]
    groups = [dict(x=x_prompt.reshape(bp * tp, d), b=bp, t=tp),
              dict(x=x_sample.reshape(bs * ts, d), b=bs, t=ts)]
    outs = [dict(C=[], n=[], m=[], pool=[]) for _ in groups]
    mk_p, mv_p = [], []
    mem = mem_prompt.reshape(bp * N_MEM, d)

    for i in range(depth):
        j = i // 2
        if i % 2 == 0:
            w_in = w_mlstm_in[j]
            w_main = w_in[:, :4 * d].astype(BF16)
            w_gate = jnp.pad(w_in[:, 4 * d:], ((0, 0), (0, LANES - 2 * HEADS)))
            bias = jnp.pad(jnp.concatenate([b_mlstm_i[j], b_mlstm_f[j]]), (0, LANES - 2 * HEADS)).reshape(1, LANES)
            w_out = w_mlstm_out[j].astype(BF16)
        else:
            w_pl = w_pool[j].astype(BF16)
        w_kv = [w_xk[i].astype(BF16), w_xv[i].astype(BF16)]
        w_q, w_o = w_xq[i].astype(BF16), w_xo[i].astype(BF16)
        w_pq = w_peer_q[i].astype(BF16)
        keys = peer_keys[i].reshape(2 * PEER_HEADS, PEER_N_KEYS, PEER_HALF).astype(BF16)
        tab_u, tab_v = _pack_table(peer_u[i]), _pack_table(peer_v[i])

        kp, vp = norm_matmul(mem, g_mem[i], w_kv, [F32, F32])
        mk_p.append(kp.reshape(bp, N_MEM, HEADS, HEAD_DIM))
        mv_p.append(vp.reshape(bp, N_MEM, HEADS, HEAD_DIM))
        kvs = [(kp.reshape(bp, N_MEM, d).astype(BF16), vp.reshape(bp, N_MEM, d).astype(BF16)),
               (cache_mem_k[i].reshape(bs, N_MEM, d).astype(BF16), cache_mem_v[i].reshape(bs, N_MEM, d).astype(BF16))]

        for gi, (grp, out) in enumerate(zip(groups, outs)):
            x, b, t = grp["x"], grp["b"], grp["t"]
            if i % 2 == 0:
                z, gz = norm_matmul(x, g_mix[i], [w_main, w_gate], [F32, F32], hi=(False, True))
                if gi == 0:
                    c0 = jnp.zeros((b, HEADS, HEAD_DIM, HEAD_DIM), F32)
                    n0 = jnp.zeros((b, HEADS, HEAD_DIM), F32)
                    m0 = jnp.zeros((b, HEADS), F32)
                else:
                    c0, n0, m0 = state_mlstm_C[j], state_mlstm_n[j], state_mlstm_m[j]
                h, c1, n1, m1 = mlstm_scan(z.reshape(b, t, 4 * d), gz.reshape(b, t, LANES), bias, c0, n0, m0)
                out["C"].append(c1); out["n"].append(n1); out["m"].append(m1)
                x = mlstm_out(h.reshape(b * t, d), z, x, g_mlstm_head[j], w_out)
            else:
                if gi == 0:
                    prefix, pos0 = jnp.zeros((b, POOL_HALO, d), F32), 0
                else:
                    prefix, pos0 = jnp.pad(state_pool[j], ((0, 0), (1, 0), (0, 0))), PAST_LEN
                x3, tail = pool_mix(x.reshape(b, t, d), prefix, pos0, g_mix[i], w_pl, s_pool[j])
                out["pool"].append(tail[:, 1:])
                x = x3.reshape(b * t, d)
            kb, vb = kvs[gi]
            x = xattn(x.reshape(b, t, d), kb, vb, g_xattn[i], w_q, w_o).reshape(b * t, d)
            x = peer(x, g_ffn[i], w_pq, keys, tab_u, tab_v)
            grp["x"] = x

    y = [rmsnorm(grp["x"], g_final).reshape(grp["b"], grp["t"], d) for grp in groups]
    st = lambda lst: jnp.stack(lst)
    po, so = outs
    return (y[0], y[1],
            st(po["C"]), st(po["n"]), st(po["m"]), st(po["pool"]), st(mk_p), st(mv_p),
            st(so["C"]), st(so["n"]), st(so["m"]), st(so["pool"]))
```
